```python
import math
import jax, jax.numpy as jnp
from jax import lax
import numpy as np

D_MODEL = 4096
BATCH = 1
SEQ = 16384
DEPTH = 4

GRID_W = 64
HEAD_DIM = 128
A_HEADS = 8
A_WIDTH = A_HEADS * 2 * HEAD_DIM
B_HEADS = 16
B_Q_LORA = 1536
B_KV_LORA = 512
B_NOPE = 128
B_ROPE = 64
B_V = 128
B_WIDTH = B_HEADS * B_V
C_HEADS = 16
C_WIDTH = C_HEADS * HEAD_DIM
NA_ROWS_MAX = 8
NA_COLS = 16

Q_BLOCK = 128
ROPE_BASE = 10000.0
EPS = 1e-6

IN_SPLITS = (A_WIDTH, A_WIDTH, A_WIDTH, A_WIDTH,
             B_Q_LORA, B_KV_LORA, B_ROPE, B_WIDTH,
             C_WIDTH, C_WIDTH, C_WIDTH, C_WIDTH,
             D_MODEL, D_MODEL, D_MODEL)
IN_COLS = sum(IN_SPLITS)
SPLIT_POINTS = tuple(int(v) for v in np.cumsum(IN_SPLITS)[:-1])

kernel_name = 'hybrid_diff_mla_neighbourhood_encoder'


def rmsnorm(x, g):
    xf = x.astype(jnp.float32)
    y = xf * lax.rsqrt(jnp.mean(xf * xf, axis=-1, keepdims=True) + EPS)
    return (y * g.astype(jnp.float32)).astype(x.dtype)


def alibi_slopes(n_heads):
    return jnp.asarray(2.0 ** (-8.0 * np.arange(1, n_heads + 1) / n_heads), dtype=jnp.float32)


def rope_tables(seq):
    inv_freq = ROPE_BASE ** (-jnp.arange(0, B_ROPE, 2, dtype=jnp.float32) / B_ROPE)
    ang = jnp.arange(seq, dtype=jnp.float32)[:, None] * inv_freq[None, :]
    return jnp.cos(ang), jnp.sin(ang)


def apply_rope(x, cos, sin):
    xf = x.astype(jnp.float32)
    x1, x2 = jnp.split(xf, 2, axis=-1)
    return jnp.concatenate([x1 * cos - x2 * sin, x2 * cos + x1 * sin], axis=-1).astype(x.dtype)


def to_blocks(t):
    b, s = t.shape[:2]
    t = t.reshape((b, s // Q_BLOCK, Q_BLOCK) + t.shape[2:])
    return jnp.moveaxis(t, 1, 0)


def from_blocks(t):
    t = jnp.moveaxis(t, 0, 1)
    return t.reshape((t.shape[0], t.shape[1] * t.shape[2]) + t.shape[3:])


def diff_attention(q, k, v, lam, lam_init, subln_g):
    s_len, d = q.shape[1], q.shape[-1]
    scale = d ** -0.5
    slopes = alibi_slopes(q.shape[2])
    pos = jnp.arange(s_len, dtype=jnp.float32)

    def block(args):
        qi, pi = args
        s = jnp.einsum('bqhmd,bkhmd->bmhqk', qi, k, preferred_element_type=jnp.float32) * scale
        dist = jnp.abs(pi[:, None] - pos[None, :])
        s = s - slopes[:, None, None] * dist[None]
        p = jax.nn.softmax(s, axis=-1)
        w = p[:, 0] - lam * p[:, 1]
        return jnp.einsum('bhqk,bkhe->bqhe', w.astype(v.dtype), v)

    o = from_blocks(lax.map(block, (to_blocks(q), pos.reshape(-1, Q_BLOCK))))
    o = rmsnorm(o, subln_g) * (1.0 - lam_init)
    return o.reshape(o.shape[:2] + (-1,))


def mla(c_q, c_kv, k_rope, q_norm_g, kv_norm_g, w_uq, w_ukv):
    b, s_len, _ = c_q.shape
    q = (rmsnorm(c_q, q_norm_g) @ w_uq).reshape(b, s_len, B_HEADS, B_NOPE + B_ROPE)
    q_nope, q_rope = q[..., :B_NOPE], q[..., B_NOPE:]
    kv = (rmsnorm(c_kv, kv_norm_g) @ w_ukv).reshape(b, s_len, B_HEADS, B_NOPE + B_V)
    k_nope, v = kv[..., :B_NOPE], kv[..., B_NOPE:]
    cos, sin = rope_tables(s_len)
    q_rope = apply_rope(q_rope, cos[:, None, :], sin[:, None, :])
    k_rope = apply_rope(k_rope, cos, sin)
    scale = (B_NOPE + B_ROPE) ** -0.5

    def block(args):
        qn, qr = args
        s = (jnp.einsum('bqhd,bkhd->bhqk', qn, k_nope, preferred_element_type=jnp.float32)
             + jnp.einsum('bqhr,bkr->bhqk', qr, k_rope, preferred_element_type=jnp.float32)) * scale
        p = jax.nn.softmax(s, axis=-1)
        return jnp.einsum('bhqk,bkhd->bqhd', p.astype(v.dtype), v)

    o = from_blocks(lax.map(block, (to_blocks(q_nope), to_blocks(q_rope))))
    return o.reshape(b, s_len, B_WIDTH)


def neighbourhood_attention(q, k, v, rpb):
    b, s_len, h, d = q.shape
    rows = s_len // GRID_W
    kr = min(NA_ROWS_MAX, rows)
    qg = q.reshape(b, rows, GRID_W, h, d)
    kg = k.reshape(b, rows, GRID_W, h, d)
    vg = v.reshape(b, rows, GRID_W, h, d)
    col = jnp.arange(GRID_W)
    col_idx = jnp.clip(col - NA_COLS // 2, 0, GRID_W - NA_COLS)[:, None] + jnp.arange(NA_COLS)[None, :]
    dc_idx = (col_idx - col[:, None] + NA_COLS - 1)[:, None, :]
    scale = d ** -0.5

    def row_fn(r):
        rs = jnp.clip(r - kr // 2, 0, rows - kr)
        k_nb = lax.dynamic_slice_in_dim(kg, rs, kr, axis=1)[:, :, col_idx]
        v_nb = lax.dynamic_slice_in_dim(vg, rs, kr, axis=1)[:, :, col_idx]
        q_row = lax.dynamic_index_in_dim(qg, r, axis=1, keepdims=False)
        s = jnp.einsum('bchd,brcwhd->bhcrw', q_row, k_nb, preferred_element_type=jnp.float32) * scale
        dr_idx = (rs + jnp.arange(kr) - r + NA_ROWS_MAX - 1)[None, :, None]
        s = s + rpb[:, dr_idx, dc_idx].astype(jnp.float32)[None]
        p = jax.nn.softmax(s.reshape(b, h, GRID_W, kr * NA_COLS), axis=-1).reshape(s.shape)
        return jnp.einsum('bhcrw,brcwhd->bchd', p.astype(v.dtype), v_nb)

    o = lax.map(row_fn, jnp.arange(rows))
    return jnp.moveaxis(o, 0, 1).reshape(b, s_len, h * d)


def setup_inputs(seed: int = 0) -> dict:
    key = jax.random.key(seed)
    ks = jax.random.split(key, 18)

    def nrm(k, shape, scale):
        return jax.random.normal(k, shape, jnp.float32) * scale

    def gain(k, shape):
        return 1.0 + 0.01 * jax.random.normal(k, shape, jnp.float32)

    return {
        'x': jax.random.normal(ks[0], (BATCH, SEQ, D_MODEL), jnp.float32),
        'norm_g': gain(ks[1], (DEPTH, D_MODEL)),
        'w_in': nrm(ks[2], (DEPTH, D_MODEL, IN_COLS), D_MODEL ** -0.5),
        'a_lam_q1': nrm(ks[3], (DEPTH, HEAD_DIM), 0.1),
        'a_lam_k1': nrm(ks[4], (DEPTH, HEAD_DIM), 0.1),
        'a_lam_q2': nrm(ks[5], (DEPTH, HEAD_DIM), 0.1),
        'a_lam_k2': nrm(ks[6], (DEPTH, HEAD_DIM), 0.1),
        'a_subln_g': gain(ks[7], (DEPTH, 2 * HEAD_DIM)),
        'b_q_norm_g': gain(ks[8], (DEPTH, B_Q_LORA)),
        'b_kv_norm_g': gain(ks[9], (DEPTH, B_KV_LORA)),
        'b_w_uq': nrm(ks[10], (DEPTH, B_Q_LORA, B_HEADS * (B_NOPE + B_ROPE)), B_Q_LORA ** -0.5),
        'b_w_ukv': nrm(ks[11], (DEPTH, B_KV_LORA, B_HEADS * (B_NOPE + B_V)), B_KV_LORA ** -0.5),
        'c_rpb': nrm(ks[12], (DEPTH, C_HEADS, 2 * NA_ROWS_MAX - 1, 2 * NA_COLS - 1), 0.1),
        'w_br_a': nrm(ks[13], (DEPTH, A_WIDTH, D_MODEL), A_WIDTH ** -0.5),
        'w_br_b': nrm(ks[14], (DEPTH, B_WIDTH, D_MODEL), B_WIDTH ** -0.5),
        'w_br_c': nrm(ks[15], (DEPTH, C_WIDTH, D_MODEL), C_WIDTH ** -0.5),
        'w_o': nrm(ks[16], (DEPTH, D_MODEL, D_MODEL), D_MODEL ** -0.5),
        'final_norm_g': gain(ks[17], (D_MODEL,)),
    }


def reference(x, norm_g, w_in, a_lam_q1, a_lam_k1, a_lam_q2, a_lam_k2, a_subln_g,
              b_q_norm_g, b_kv_norm_g, b_w_uq, b_w_ukv, c_rpb, w_br_a, w_br_b, w_br_c,
              w_o, final_norm_g):
    b, s_len, _ = x.shape
    for l in range(DEPTH):
        h = rmsnorm(x, norm_g[l])
        z = h @ w_in[l]
        (qa, ka, va, ga, cq, ckv, krope, gb, qc, kc, vc, gc, sa, sb, sc) = jnp.split(z, SPLIT_POINTS, axis=-1)

        lam_init = 0.8 - 0.6 * math.exp(-0.3 * l)
        lam = (jnp.exp(jnp.sum(a_lam_q1[l].astype(jnp.float32) * a_lam_k1[l].astype(jnp.float32)))
               - jnp.exp(jnp.sum(a_lam_q2[l].astype(jnp.float32) * a_lam_k2[l].astype(jnp.float32)))
               + lam_init)
        ya = diff_attention(qa.reshape(b, s_len, A_HEADS, 2, HEAD_DIM),
                            ka.reshape(b, s_len, A_HEADS, 2, HEAD_DIM),
                            va.reshape(b, s_len, A_HEADS, 2 * HEAD_DIM),
                            lam, lam_init, a_subln_g[l]) * jax.nn.silu(ga)

        yb = mla(cq, ckv, krope, b_q_norm_g[l], b_kv_norm_g[l], b_w_uq[l], b_w_ukv[l]) * jax.nn.silu(gb)

        yc = neighbourhood_attention(qc.reshape(b, s_len, C_HEADS, HEAD_DIM),
                                     kc.reshape(b, s_len, C_HEADS, HEAD_DIM),
                                     vc.reshape(b, s_len, C_HEADS, HEAD_DIM),
                                     c_rpb[l]) * jax.nn.silu(gc)

        merged = (jax.nn.sigmoid(sa) * (ya @ w_br_a[l])
                  + jax.nn.sigmoid(sb) * (yb @ w_br_b[l])
                  + jax.nn.sigmoid(sc) * (yc @ w_br_c[l]))
        x = x + merged @ w_o[l]
    return rmsnorm(x, final_norm_g)
```

```python
import functools
import math

import numpy as np
import jax
import jax.numpy as jnp
from jax import lax
from jax.experimental import pallas as pl
from jax.experimental.pallas import tpu as pltpu

F32 = jnp.float32
BF16 = jnp.bfloat16

D_MODEL = 4096
SEQ = 16384
DEPTH = 4
GRID_W = 64
HEAD_DIM = 128
A_HEADS = 8
A_WIDTH = A_HEADS * 2 * HEAD_DIM
B_HEADS = 16
B_Q_LORA = 1536
B_KV_LORA = 512
B_NOPE = 128
B_ROPE = 64
B_V = 128
B_WIDTH = B_HEADS * B_V
C_HEADS = 16
C_WIDTH = C_HEADS * HEAD_DIM
NA_ROWS = 8
NA_COLS = 16
ROPE_BASE = 10000.0
EPS = 1e-6

LANES = 128
MIB = 1024 * 1024
NEG = -1e30

OFF_CQ = 0
OFF_CKV = OFF_CQ + B_Q_LORA
OFF_QA = OFF_CKV + B_KV_LORA
OFF_KA = OFF_QA + A_WIDTH
OFF_VA = OFF_KA + A_WIDTH
OFF_GA = OFF_VA + A_WIDTH
OFF_SA = OFF_GA + A_WIDTH
OFF_SB = OFF_SA + D_MODEL
OFF_SC = OFF_SB + D_MODEL
OFF_QC = OFF_SC + D_MODEL
OFF_KC = OFF_QC + C_WIDTH
OFF_VC = OFF_KC + C_WIDTH
OFF_GC = OFF_VC + C_WIDTH
OFF_GB = OFF_GC + C_WIDTH
OFF_KR = OFF_GB + B_WIDTH
Z_USED = OFF_KR + 2 * B_ROPE
Z_TN = 512
Z_COLS = -(-Z_USED // Z_TN) * Z_TN

B_QK = 2 * LANES


def _params(vmem_mib, ndims):
    return pltpu.CompilerParams(dimension_semantics=("arbitrary",) * ndims,
                                vmem_limit_bytes=vmem_mib * MIB)


def _rms(x, g):
    ms = jnp.mean(x * x, axis=-1, keepdims=True)
    return x * lax.rsqrt(ms + EPS) * g


def _rmsnorm_kernel(x_ref, g_ref, o_ref):
    o_ref[...] = _rms(x_ref[...], g_ref[...]).astype(o_ref.dtype)


def _rmsnorm(x, g, out_dtype, tm=256):
    m, d = x.shape
    return pl.pallas_call(
        _rmsnorm_kernel,
        grid=(m // tm,),
        in_specs=[pl.BlockSpec((tm, d), lambda i: (i, 0)),
                  pl.BlockSpec((1, d), lambda i: (0, 0))],
        out_specs=pl.BlockSpec((tm, d), lambda i: (i, 0)),
        out_shape=jax.ShapeDtypeStruct((m, d), out_dtype),
        compiler_params=_params(40, 1),
        name="rmsnorm",
    )(x, g.reshape(1, d))


def _mm_kernel(a_ref, b_ref, o_ref):
    o_ref[...] = jnp.dot(a_ref[...], b_ref[...],
                         preferred_element_type=F32).astype(o_ref.dtype)


def _matmul(a, b, out_dtype, tm, tn, name):
    m, k = a.shape
    n = b.shape[1]
    return pl.pallas_call(
        _mm_kernel,
        grid=(m // tm, n // tn),
        in_specs=[pl.BlockSpec((tm, k), lambda i, j: (i, 0)),
                  pl.BlockSpec((k, tn), lambda i, j: (0, j))],
        out_specs=pl.BlockSpec((tm, tn), lambda i, j: (i, j)),
        out_shape=jax.ShapeDtypeStruct((m, n), out_dtype),
        compiler_params=_params(48, 2),
        name=name,
    )(a, b)


def _mm_res_kernel(a_ref, b_ref, x_ref, o_ref):
    o_ref[...] = x_ref[...] + jnp.dot(a_ref[...], b_ref[...],
                                      preferred_element_type=F32)


def _matmul_residual(a, b, x, tm, tn):
    m, k = a.shape
    n = b.shape[1]
    return pl.pallas_call(
        _mm_res_kernel,
        grid=(m // tm, n // tn),
        in_specs=[pl.BlockSpec((tm, k), lambda i, j: (i, 0)),
                  pl.BlockSpec((k, tn), lambda i, j: (0, j)),
                  pl.BlockSpec((tm, tn), lambda i, j: (i, j))],
        out_specs=pl.BlockSpec((tm, tn), lambda i, j: (i, j)),
        out_shape=jax.ShapeDtypeStruct((m, n), F32),
        compiler_params=_params(48, 2),
        name="out_proj",
    )(a, b, x)


def _rope128(u, cos_t, sin_t):
    return u * cos_t + pltpu.roll(u, 2 * (B_ROPE // 2), 1) * sin_t


def _qup_kernel(cq_ref, g_ref, w_ref, cos_ref, sin_ref, o_ref, *, heads, scale):
    hn = _rms(cq_ref[...].astype(F32), g_ref[...]).astype(BF16)
    acc = jnp.dot(hn, w_ref[...], preferred_element_type=F32)
    cos_t = cos_ref[...]
    sin_t = sin_ref[...]
    for h in range(heads):
        lo = acc[:, h * B_QK:h * B_QK + LANES]
        up = acc[:, h * B_QK + LANES:(h + 1) * B_QK]
        o_ref[:, h * B_QK:h * B_QK + LANES] = (lo * scale).astype(BF16)
        o_ref[:, h * B_QK + LANES:(h + 1) * B_QK] = (
            _rope128(up, cos_t, sin_t) * scale).astype(BF16)


def _q_up(z, g, w, cos_t, sin_t, tm=512, heads=4):
    tn = heads * B_QK
    n = B_HEADS * B_QK
    scale = (B_NOPE + B_ROPE) ** -0.5
    return pl.pallas_call(
        functools.partial(_qup_kernel, heads=heads, scale=scale),
        grid=(SEQ // tm, n // tn),
        in_specs=[pl.BlockSpec((tm, B_Q_LORA), lambda i, j: (i, OFF_CQ // B_Q_LORA)),
                  pl.BlockSpec((1, B_Q_LORA), lambda i, j: (0, 0)),
                  pl.BlockSpec((B_Q_LORA, tn), lambda i, j: (0, j)),
                  pl.BlockSpec((tm, LANES), lambda i, j: (i, 0)),
                  pl.BlockSpec((tm, LANES), lambda i, j: (i, 0))],
        out_specs=pl.BlockSpec((tm, tn), lambda i, j: (i, j)),
        out_shape=jax.ShapeDtypeStruct((SEQ, n), BF16),
        compiler_params=_params(40, 2),
        name="mla_q_up",
    )(z, g.reshape(1, B_Q_LORA), w, cos_t, sin_t)


def _kvup_kernel(ckv_ref, kr_ref, g_ref, wk_ref, wv_ref, cos_ref, sin_ref,
                 k_ref, v_ref):
    hn = _rms(ckv_ref[...].astype(F32), g_ref[...]).astype(BF16)
    kn = jnp.dot(hn, wk_ref[...], preferred_element_type=F32)
    v_ref[...] = jnp.dot(hn, wv_ref[...], preferred_element_type=F32).astype(BF16)
    kr = _rope128(kr_ref[...].astype(F32), cos_ref[...], sin_ref[...]).astype(BF16)
    for h in range(B_HEADS):
        k_ref[:, h * B_QK:h * B_QK + LANES] = kn[:, h * B_NOPE:(h + 1) * B_NOPE].astype(BF16)
        k_ref[:, h * B_QK + LANES:(h + 1) * B_QK] = kr


def _kv_up(z, g, wk, wv, cos_t, sin_t, tm=512):
    return pl.pallas_call(
        _kvup_kernel,
        grid=(SEQ // tm,),
        in_specs=[pl.BlockSpec((tm, B_KV_LORA), lambda i: (i, OFF_CKV // B_KV_LORA)),
                  pl.BlockSpec((tm, LANES), lambda i: (i, OFF_KR // LANES)),
                  pl.BlockSpec((1, B_KV_LORA), lambda i: (0, 0)),
                  pl.BlockSpec((B_KV_LORA, B_HEADS * B_NOPE), lambda i: (0, 0)),
                  pl.BlockSpec((B_KV_LORA, B_WIDTH), lambda i: (0, 0)),
                  pl.BlockSpec((tm, LANES), lambda i: (i, 0)),
                  pl.BlockSpec((tm, LANES), lambda i: (i, 0))],
        out_specs=[pl.BlockSpec((tm, B_HEADS * B_QK), lambda i: (i, 0)),
                   pl.BlockSpec((tm, B_WIDTH), lambda i: (i, 0))],
        out_shape=[jax.ShapeDtypeStruct((SEQ, B_HEADS * B_QK), BF16),
                   jax.ShapeDtypeStruct((SEQ, B_WIDTH), BF16)],
        compiler_params=_params(40, 1),
        name="mla_kv_up",
    )(z, z, g.reshape(1, B_KV_LORA), wk, wv, cos_t, sin_t)


def _flash_loop(q, k_ref, k_cols, v_ref, m_ref, l_ref, acc_ref, tk, bias_fn):
    nk = k_ref.shape[0] // tk
    m_ref[...] = jnp.full(m_ref.shape, -jnp.inf, F32)
    l_ref[...] = jnp.zeros(l_ref.shape, F32)
    acc_ref[...] = jnp.zeros(acc_ref.shape, F32)

    def body(ki, carry):
        k0 = pl.multiple_of(ki * tk, tk)
        kc = k_ref[pl.ds(k0, tk), k_cols]
        s = lax.dot_general(q, kc, (((1,), (1,)), ((), ())),
                            preferred_element_type=F32)
        if bias_fn is not None:
            s = s - bias_fn(k0)
        m_prev = m_ref[...]
        m_new = jnp.maximum(m_prev, jnp.max(s, axis=1, keepdims=True))
        alpha = jnp.exp(m_prev - m_new)
        p = jnp.exp(s - m_new)
        l_ref[...] = alpha * l_ref[...] + jnp.sum(p, axis=1, keepdims=True)
        acc_ref[...] = alpha * acc_ref[...] + jnp.dot(
            p.astype(BF16), v_ref[pl.ds(k0, tk), :], preferred_element_type=F32)
        m_ref[...] = m_new
        return carry

    lax.fori_loop(0, nk, body, 0)


def _silu(g):
    return g * jax.nn.sigmoid(g)


def _flash_b_kernel(q_ref, k_ref, v_ref, g_ref, o_ref, m_ref, l_ref, acc_ref, *, tk):
    _flash_loop(q_ref[...], k_ref, slice(None), v_ref, m_ref, l_ref, acc_ref, tk, None)
    o = acc_ref[...] * (1.0 / l_ref[...])
    o_ref[...] = (o * _silu(g_ref[...].astype(F32))).astype(BF16)


def _flash_b(qb, kb, vb, z, tq=512, tk=2048):
    return pl.pallas_call(
        functools.partial(_flash_b_kernel, tk=tk),
        grid=(B_HEADS, SEQ // tq),
        in_specs=[pl.BlockSpec((tq, B_QK), lambda h, i: (i, h)),
                  pl.BlockSpec((SEQ, B_QK), lambda h, i: (0, h)),
                  pl.BlockSpec((SEQ, B_V), lambda h, i: (0, h)),
                  pl.BlockSpec((tq, B_V), lambda h, i: (i, OFF_GB // B_V + h))],
        out_specs=pl.BlockSpec((tq, B_V), lambda h, i: (i, h)),
        out_shape=jax.ShapeDtypeStruct((SEQ, B_WIDTH), BF16),
        scratch_shapes=[pltpu.VMEM((tq, 1), F32), pltpu.VMEM((tq, 1), F32),
                        pltpu.VMEM((tq, B_V), F32)],
        compiler_params=_params(56, 2),
        name="flash_mla",
    )(qb, kb, vb, z)


def _flash_a_kernel(slopes_ref, laminit_ref, q_ref, k_ref, v_ref, g_ref, subg_ref,
                    lamv_ref, o_ref, m_ref, l_ref, acc_ref, *, tq, tk, scale):
    h = pl.program_id(0)
    q0 = pl.program_id(1) * tq
    slope = slopes_ref[h]
    rel = (lax.broadcasted_iota(jnp.int32, (tq, tk), 0)
           - lax.broadcasted_iota(jnp.int32, (tq, tk), 1))

    def bias_fn(k0):
        return slope * jnp.abs(rel + (q0 - k0)).astype(F32)

    outs = []
    for mp in range(2):
        q = (q_ref[:, mp * HEAD_DIM:(mp + 1) * HEAD_DIM].astype(F32) * scale).astype(BF16)
        _flash_loop(q, k_ref, slice(mp * HEAD_DIM, (mp + 1) * HEAD_DIM), v_ref,
                    m_ref, l_ref, acc_ref.at[mp], tk, bias_fn)
        outs.append(acc_ref[mp] * (1.0 / l_ref[...]))

    lam_init = laminit_ref[0]
    lv = lamv_ref[...]
    lam = (jnp.exp(jnp.sum(lv[0:1] * lv[1:2], axis=1, keepdims=True))
           - jnp.exp(jnp.sum(lv[2:3] * lv[3:4], axis=1, keepdims=True)) + lam_init)
    o = outs[0] - lam * outs[1]
    o = _rms(o, subg_ref[...]) * (1.0 - lam_init)
    o_ref[...] = (o * _silu(g_ref[...].astype(F32))).astype(BF16)


def _flash_a(z, slopes, lam_init, subln_g, lam_vecs, tq=512, tk=2048):
    w = 2 * HEAD_DIM
    smem = pl.BlockSpec(memory_space=pltpu.SMEM)
    return pl.pallas_call(
        functools.partial(_flash_a_kernel, tq=tq, tk=tk, scale=HEAD_DIM ** -0.5),
        grid=(A_HEADS, SEQ // tq),
        in_specs=[smem, smem,
                  pl.BlockSpec((tq, w), lambda h, i: (i, OFF_QA // w + h)),
                  pl.BlockSpec((SEQ, w), lambda h, i: (0, OFF_KA // w + h)),
                  pl.BlockSpec((SEQ, w), lambda h, i: (0, OFF_VA // w + h)),
                  pl.BlockSpec((tq, w), lambda h, i: (i, OFF_GA // w + h)),
                  pl.BlockSpec((1, w), lambda h, i: (0, 0)),
                  pl.BlockSpec((4, HEAD_DIM), lambda h, i: (0, 0))],
        out_specs=pl.BlockSpec((tq, w), lambda h, i: (i, h)),
        out_shape=jax.ShapeDtypeStruct((SEQ, A_WIDTH), BF16),
        scratch_shapes=[pltpu.VMEM((tq, 1), F32), pltpu.VMEM((tq, 1), F32),
                        pltpu.VMEM((2, tq, w), F32)],
        compiler_params=_params(58, 2),
        name="flash_diff",
    )(slopes, lam_init, z, z, z, z, subln_g.reshape(1, w), lam_vecs)


NA_GROUP_ROWS = 8
NA_Q = NA_GROUP_ROWS * GRID_W
NA_KROWS = 2 * NA_GROUP_ROWS
NA_K = NA_KROWS * GRID_W
NA_KBLK = 256
NA_GROUPS = SEQ // NA_Q
N_DR = 2 * NA_ROWS - 1
N_DC = 2 * NA_COLS - 1
NA_VARIANTS = (
    (0, lambda rq: max(rq - NA_ROWS // 2, 0)),
    (-NA_ROWS // 2, lambda rq: rq),
    (-NA_ROWS, lambda rq: min(rq + NA_ROWS // 2, NA_ROWS)),
)


def _na_bias_kernel(rpb_ref, o_ref):
    h = pl.program_id(0)
    lane = lax.broadcasted_iota(jnp.int32, (GRID_W, LANES), 1)
    c = lax.broadcasted_iota(jnp.int32, (GRID_W, LANES), 0)
    cp = lane & (GRID_W - 1)
    cs = jnp.clip(c - NA_COLS // 2, 0, GRID_W - NA_COLS)
    col_valid = (cp >= cs) & (cp < cs + NA_COLS)
    dc = cp - c + (NA_COLS - 1)
    neg = jnp.full((GRID_W, LANES), NEG, F32)
    base = h * (N_DR * N_DC)
    tabs = []
    for dr in range(N_DR):
        def jb(j, acc, dr=dr):
            return jnp.where(dc == j, rpb_ref[base + dr * N_DC + j], acc)
        t = lax.fori_loop(0, N_DC, jb, neg)
        tabs.append(jnp.where(col_valid, t, neg))
    low = lane < GRID_W
    for var, (delta, rs_fn) in enumerate(NA_VARIANTS):
        for rq in range(NA_GROUP_ROWS):
            rs = rs_fn(rq)
            for pair in range(NA_KROWS // 2):
                halves = []
                for rk in (2 * pair, 2 * pair + 1):
                    if rs <= rk < rs + NA_ROWS:
                        halves.append(tabs[rk - rq + delta + NA_ROWS - 1])
                    else:
                        halves.append(neg)
                o_ref[0, var, rq * GRID_W:(rq + 1) * GRID_W,
                      pair * LANES:(pair + 1) * LANES] = jnp.where(low, halves[0], halves[1])


def _na_bias(rpb):
    return pl.pallas_call(
        _na_bias_kernel,
        grid=(C_HEADS,),
        in_specs=[pl.BlockSpec(memory_space=pltpu.SMEM)],
        out_specs=pl.BlockSpec((1, len(NA_VARIANTS), NA_Q, NA_K), lambda h: (h, 0, 0, 0)),
        out_shape=jax.ShapeDtypeStruct((C_HEADS, len(NA_VARIANTS), NA_Q, NA_K), F32),
        compiler_params=_params(32, 1),
        name="na_bias",
    )(rpb.reshape(-1))


def _na_kernel(q_ref, k0_ref, k1_ref, k2_ref, k3_ref, v0_ref, v1_ref, v2_ref, v3_ref,
               b_ref, g_ref, o_ref, *, scale):
    q = (q_ref[...].astype(F32) * scale).astype(BF16)
    s = jnp.concatenate(
        [lax.dot_general(q, kr[...], (((1,), (1,)), ((), ())), preferred_element_type=F32)
         for kr in (k0_ref, k1_ref, k2_ref, k3_ref)], axis=1) + b_ref[0, 0]
    m = jnp.max(s, axis=1, keepdims=True)
    p = jnp.exp(s - m)
    l = jnp.sum(p, axis=1, keepdims=True)
    pb = p.astype(BF16)
    o = None
    for i, vr in enumerate((v0_ref, v1_ref, v2_ref, v3_ref)):
        t = jnp.dot(pb[:, i * NA_KBLK:(i + 1) * NA_KBLK], vr[...], preferred_element_type=F32)
        o = t if o is None else o + t
    o = o * (1.0 / l)
    o_ref[...] = (o * _silu(g_ref[...].astype(F32))).astype(BF16)


def _na(z, bias):
    d = HEAD_DIM
    last_start = (SEQ - NA_K) // NA_KBLK

    def kstart(g):
        return jnp.clip(2 * g - 1, 0, last_start)

    def kv_spec(off, i):
        return pl.BlockSpec((NA_KBLK, d), lambda h, g: (kstart(g) + i, off // d + h))

    def variant(g):
        return jnp.where(g == 0, 0, jnp.where(g == NA_GROUPS - 1, 2, 1))

    return pl.pallas_call(
        functools.partial(_na_kernel, scale=d ** -0.5),
        grid=(C_HEADS, NA_GROUPS),
        in_specs=([pl.BlockSpec((NA_Q, d), lambda h, g: (g, OFF_QC // d + h))]
                  + [kv_spec(OFF_KC, i) for i in range(4)]
                  + [kv_spec(OFF_VC, i) for i in range(4)]
                  + [pl.BlockSpec((1, 1, NA_Q, NA_K), lambda h, g: (h, variant(g), 0, 0)),
                     pl.BlockSpec((NA_Q, d), lambda h, g: (g, OFF_GC // d + h))]),
        out_specs=pl.BlockSpec((NA_Q, d), lambda h, g: (g, h)),
        out_shape=jax.ShapeDtypeStruct((SEQ, C_WIDTH), BF16),
        compiler_params=_params(40, 2),
        name="na_attn",
    )(z, z, z, z, z, z, z, z, z, bias, z)


def _merge_kernel(ya_ref, yb_ref, yc_ref, wa_ref, wb_ref, wc_ref,
                  sa_ref, sb_ref, sc_ref, o_ref):
    def branch(y_ref, w_ref, s_ref):
        return jax.nn.sigmoid(s_ref[...].astype(F32)) * jnp.dot(
            y_ref[...], w_ref[...], preferred_element_type=F32)
    o_ref[...] = (branch(ya_ref, wa_ref, sa_ref) + branch(yb_ref, wb_ref, sb_ref)
                  + branch(yc_ref, wc_ref, sc_ref)).astype(BF16)


def _merge(ya, yb, yc, wa, wb, wc, z, tm=512, tn=512):
    def y_spec(width):
        return pl.BlockSpec((tm, width), lambda i, j: (i, 0))

    def w_spec(width):
        return pl.BlockSpec((width, tn), lambda i, j: (0, j))

    def s_spec(off):
        return pl.BlockSpec((tm, tn), lambda i, j: (i, off // tn + j))

    return pl.pallas_call(
        _merge_kernel,
        grid=(SEQ // tm, D_MODEL // tn),
        in_specs=[y_spec(A_WIDTH), y_spec(B_WIDTH), y_spec(C_WIDTH),
                  w_spec(A_WIDTH), w_spec(B_WIDTH), w_spec(C_WIDTH),
                  s_spec(OFF_SA), s_spec(OFF_SB), s_spec(OFF_SC)],
        out_specs=pl.BlockSpec((tm, tn), lambda i, j: (i, j)),
        out_shape=jax.ShapeDtypeStruct((SEQ, D_MODEL), BF16),
        compiler_params=_params(48, 2),
        name="merge",
    )(ya, yb, yc, wa, wb, wc, z, z, z)


def _swap_halves(w):
    half = w.shape[-1] // 2
    return jnp.concatenate([w[..., half:], w[..., :half]], axis=-1)


def _prep_w_in(w):
    o = np.cumsum((0,) + (A_WIDTH,) * 4 + (B_Q_LORA, B_KV_LORA, B_ROPE, B_WIDTH)
                  + (C_WIDTH,) * 4 + (D_MODEL,) * 3)
    a_all = w[:, o[0]:o[4]]
    cq = w[:, o[4]:o[5]]
    ckv = w[:, o[5]:o[6]]
    kr = w[:, o[6]:o[7]]
    gb = w[:, o[7]:o[8]]
    c_all = w[:, o[8]:o[12]]
    gates = w[:, o[12]:o[15]]
    pad = jnp.zeros((w.shape[0], Z_COLS - Z_USED), w.dtype)
    return jnp.concatenate([cq, ckv, a_all, gates, c_all, gb, kr, _swap_halves(kr), pad],
                           axis=1).astype(BF16)


def _prep_w_uq(w):
    w = w.reshape(B_Q_LORA, B_HEADS, B_NOPE + B_ROPE)
    rope = w[..., B_NOPE:]
    return jnp.concatenate([w[..., :B_NOPE], rope, _swap_halves(rope)],
                           axis=-1).reshape(B_Q_LORA, B_HEADS * B_QK).astype(BF16)


def _prep_w_ukv(w):
    w = w.reshape(B_KV_LORA, B_HEADS, B_NOPE + B_V)
    wk = w[..., :B_NOPE].reshape(B_KV_LORA, B_HEADS * B_NOPE).astype(BF16)
    wv = w[..., B_NOPE:].reshape(B_KV_LORA, B_WIDTH).astype(BF16)
    return wk, wv


def _rope_tables():
    inv_freq = ROPE_BASE ** (-jnp.arange(0, B_ROPE, 2, dtype=F32) / B_ROPE)
    ang = jnp.arange(SEQ, dtype=F32)[:, None] * inv_freq[None, :]
    cos, sin = jnp.cos(ang), jnp.sin(ang)
    zero = jnp.zeros((SEQ, LANES - B_ROPE), F32)
    return (jnp.concatenate([cos, cos, zero], axis=1),
            jnp.concatenate([-sin, sin, zero], axis=1))


def kernel(x, norm_g, w_in, a_lam_q1, a_lam_k1, a_lam_q2, a_lam_k2, a_subln_g,
           b_q_norm_g, b_kv_norm_g, b_w_uq, b_w_ukv, c_rpb, w_br_a, w_br_b, w_br_c,
           w_o, final_norm_g):
    assert x.shape == (1, SEQ, D_MODEL)
    cos_t, sin_t = _rope_tables()
    slopes = jnp.asarray(2.0 ** (-8.0 * np.arange(1, A_HEADS + 1) / A_HEADS), dtype=F32)
    lam_inits = jnp.asarray([0.8 - 0.6 * math.exp(-0.3 * l) for l in range(DEPTH)],
                            dtype=F32).reshape(DEPTH, 1)
    lam_vecs = jnp.stack([a_lam_q1, a_lam_k1, a_lam_q2, a_lam_k2], axis=1).astype(F32)

    def layer(xc, p):
        (g_l, w_in_l, lam_init_l, lam_vecs_l, subg_l, qg_l, kvg_l, wuq_l, wukv_l,
         rpb_l, wa_l, wb_l, wc_l, wo_l) = p
        h = _rmsnorm(xc, g_l, BF16)
        z = _matmul(h, _prep_w_in(w_in_l), BF16, 1024, Z_TN, "in_proj")
        ya = _flash_a(z, slopes, lam_init_l, subg_l, lam_vecs_l)
        qb = _q_up(z, qg_l, _prep_w_uq(wuq_l), cos_t, sin_t)
        wk, wv = _prep_w_ukv(wukv_l)
        kb, vb = _kv_up(z, kvg_l, wk, wv, cos_t, sin_t)
        yb = _flash_b(qb, kb, vb, z)
        yc = _na(z, _na_bias(rpb_l))
        merged = _merge(ya, yb, yc, wa_l.astype(BF16), wb_l.astype(BF16),
                        wc_l.astype(BF16), z)
        return _matmul_residual(merged, wo_l.astype(BF16), xc, 512, 1024), None

    xs = (norm_g, w_in, lam_inits, lam_vecs, a_subln_g, b_q_norm_g, b_kv_norm_g,
          b_w_uq, b_w_ukv, c_rpb, w_br_a, w_br_b, w_br_c, w_o)
    xf, _ = lax.scan(layer, x[0], xs)
    return _rmsnorm(xf, final_norm_g, F32)[None]
```

```python
import functools
import math

import numpy as np
import jax
import jax.numpy as jnp
from jax import lax
from jax.experimental import pallas as pl
from jax.experimental.pallas import tpu as pltpu

F32 = jnp.float32
BF16 = jnp.bfloat16

D_MODEL = 4096
SEQ = 16384
DEPTH = 4
GRID_W = 64
HEAD_DIM = 128
A_HEADS = 8
A_WIDTH = A_HEADS * 2 * HEAD_DIM
B_HEADS = 16
B_Q_LORA = 1536
B_KV_LORA = 512
B_NOPE = 128
B_ROPE = 64
B_V = 128
B_WIDTH = B_HEADS * B_V
C_HEADS = 16
C_WIDTH = C_HEADS * HEAD_DIM
NA_ROWS = 8
NA_COLS = 16
ROPE_BASE = 10000.0
EPS = 1e-6

LANES = 128
MIB = 1024 * 1024
NEG = -1e30

OFF_CQ = 0
OFF_CKV = OFF_CQ + B_Q_LORA
OFF_QA = OFF_CKV + B_KV_LORA
OFF_KA = OFF_QA + A_WIDTH
OFF_VA = OFF_KA + A_WIDTH
OFF_GA = OFF_VA + A_WIDTH
OFF_SA = OFF_GA + A_WIDTH
OFF_SB = OFF_SA + D_MODEL
OFF_SC = OFF_SB + D_MODEL
OFF_QC = OFF_SC + D_MODEL
OFF_KC = OFF_QC + C_WIDTH
OFF_VC = OFF_KC + C_WIDTH
OFF_GC = OFF_VC + C_WIDTH
OFF_GB = OFF_GC + C_WIDTH
OFF_KR = OFF_GB + B_WIDTH
Z_USED = OFF_KR + 2 * B_ROPE
Z_TN = 512
Z_COLS = -(-Z_USED // Z_TN) * Z_TN

B_QK = 2 * LANES
B_VP = 2 * LANES
LOG2E = math.log2(math.e)


def _params(vmem_mib, ndims):
    return pltpu.CompilerParams(dimension_semantics=("arbitrary",) * ndims,
                                vmem_limit_bytes=vmem_mib * MIB)


def _rms(x, g):
    ms = jnp.mean(x * x, axis=-1, keepdims=True)
    return x * lax.rsqrt(ms + EPS) * g


def _rmsnorm_kernel(x_ref, g_ref, o_ref):
    o_ref[...] = _rms(x_ref[...], g_ref[...]).astype(o_ref.dtype)


def _rmsnorm(x, g, out_dtype, tm=256):
    m, d = x.shape
    return pl.pallas_call(
        _rmsnorm_kernel,
        grid=(m // tm,),
        in_specs=[pl.BlockSpec((tm, d), lambda i: (i, 0)),
                  pl.BlockSpec((1, d), lambda i: (0, 0))],
        out_specs=pl.BlockSpec((tm, d), lambda i: (i, 0)),
        out_shape=jax.ShapeDtypeStruct((m, d), out_dtype),
        compiler_params=_params(40, 1),
        name="rmsnorm",
    )(x, g.reshape(1, d))


def _mm_kernel(a_ref, b_ref, o_ref):
    o_ref[...] = jnp.dot(a_ref[...], b_ref[...],
                         preferred_element_type=F32).astype(o_ref.dtype)


def _matmul(a, b, out_dtype, tm, tn, name):
    m, k = a.shape
    n = b.shape[1]
    return pl.pallas_call(
        _mm_kernel,
        grid=(m // tm, n // tn),
        in_specs=[pl.BlockSpec((tm, k), lambda i, j: (i, 0)),
                  pl.BlockSpec((k, tn), lambda i, j: (0, j))],
        out_specs=pl.BlockSpec((tm, tn), lambda i, j: (i, j)),
        out_shape=jax.ShapeDtypeStruct((m, n), out_dtype),
        compiler_params=_params(48, 2),
        name=name,
    )(a, b)


def _mm_res_kernel(a_ref, b_ref, x_ref, o_ref):
    o_ref[...] = x_ref[...] + jnp.dot(a_ref[...], b_ref[...],
                                      preferred_element_type=F32)


def _matmul_residual(a, b, x, tm, tn):
    m, k = a.shape
    n = b.shape[1]
    return pl.pallas_call(
        _mm_res_kernel,
        grid=(m // tm, n // tn),
        in_specs=[pl.BlockSpec((tm, k), lambda i, j: (i, 0)),
                  pl.BlockSpec((k, tn), lambda i, j: (0, j)),
                  pl.BlockSpec((tm, tn), lambda i, j: (i, j))],
        out_specs=pl.BlockSpec((tm, tn), lambda i, j: (i, j)),
        out_shape=jax.ShapeDtypeStruct((m, n), F32),
        compiler_params=_params(48, 2),
        name="out_proj",
    )(a, b, x)


def _rope128(u, cos_t, sin_t):
    return u * cos_t + pltpu.roll(u, 2 * (B_ROPE // 2), 1) * sin_t


def _qup_kernel(cq_ref, g_ref, w_ref, cos_ref, sin_ref, o_ref, *, heads, scale):
    hn = _rms(cq_ref[...].astype(F32), g_ref[...]).astype(BF16)
    acc = jnp.dot(hn, w_ref[...], preferred_element_type=F32)
    cos_t = cos_ref[...]
    sin_t = sin_ref[...]
    for h in range(heads):
        lo = acc[:, h * B_QK:h * B_QK + LANES]
        up = acc[:, h * B_QK + LANES:(h + 1) * B_QK]
        o_ref[:, h * B_QK:h * B_QK + LANES] = (lo * scale).astype(BF16)
        o_ref[:, h * B_QK + LANES:(h + 1) * B_QK] = (
            _rope128(up, cos_t, sin_t) * scale).astype(BF16)


def _q_up(z, g, w, cos_t, sin_t, tm=512, heads=4):
    tn = heads * B_QK
    n = B_HEADS * B_QK
    scale = (B_NOPE + B_ROPE) ** -0.5 * LOG2E
    return pl.pallas_call(
        functools.partial(_qup_kernel, heads=heads, scale=scale),
        grid=(SEQ // tm, n // tn),
        in_specs=[pl.BlockSpec((tm, B_Q_LORA), lambda i, j: (i, OFF_CQ // B_Q_LORA)),
                  pl.BlockSpec((1, B_Q_LORA), lambda i, j: (0, 0)),
                  pl.BlockSpec((B_Q_LORA, tn), lambda i, j: (0, j)),
                  pl.BlockSpec((tm, LANES), lambda i, j: (i, 0)),
                  pl.BlockSpec((tm, LANES), lambda i, j: (i, 0))],
        out_specs=pl.BlockSpec((tm, tn), lambda i, j: (i, j)),
        out_shape=jax.ShapeDtypeStruct((SEQ, n), BF16),
        compiler_params=_params(40, 2),
        name="mla_q_up",
    )(z, g.reshape(1, B_Q_LORA), w, cos_t, sin_t)


def _kvup_kernel(ckv_ref, kr_ref, g_ref, wk_ref, wv_ref, cos_ref, sin_ref,
                 k_ref, v_ref):
    hn = _rms(ckv_ref[...].astype(F32), g_ref[...]).astype(BF16)
    kn = jnp.dot(hn, wk_ref[...], preferred_element_type=F32)
    vn = jnp.dot(hn, wv_ref[...], preferred_element_type=F32)
    kr = _rope128(kr_ref[...].astype(F32), cos_ref[...], sin_ref[...]).astype(BF16)
    lane = lax.broadcasted_iota(jnp.int32, (kn.shape[0], LANES), 1)
    ones_col = jnp.where(lane == 0, 1.0, 0.0).astype(BF16)
    for h in range(B_HEADS):
        k_ref[:, h * B_QK:h * B_QK + LANES] = kn[:, h * B_NOPE:(h + 1) * B_NOPE].astype(BF16)
        k_ref[:, h * B_QK + LANES:(h + 1) * B_QK] = kr
        v_ref[:, h * B_VP:h * B_VP + B_V] = vn[:, h * B_V:(h + 1) * B_V].astype(BF16)
        v_ref[:, h * B_VP + B_V:(h + 1) * B_VP] = ones_col


def _kv_up(z, g, wk, wv, cos_t, sin_t, tm=512):
    return pl.pallas_call(
        _kvup_kernel,
        grid=(SEQ // tm,),
        in_specs=[pl.BlockSpec((tm, B_KV_LORA), lambda i: (i, OFF_CKV // B_KV_LORA)),
                  pl.BlockSpec((tm, LANES), lambda i: (i, OFF_KR // LANES)),
                  pl.BlockSpec((1, B_KV_LORA), lambda i: (0, 0)),
                  pl.BlockSpec((B_KV_LORA, B_HEADS * B_NOPE), lambda i: (0, 0)),
                  pl.BlockSpec((B_KV_LORA, B_WIDTH), lambda i: (0, 0)),
                  pl.BlockSpec((tm, LANES), lambda i: (i, 0)),
                  pl.BlockSpec((tm, LANES), lambda i: (i, 0))],
        out_specs=[pl.BlockSpec((tm, B_HEADS * B_QK), lambda i: (i, 0)),
                   pl.BlockSpec((tm, B_HEADS * B_VP), lambda i: (i, 0))],
        out_shape=[jax.ShapeDtypeStruct((SEQ, B_HEADS * B_QK), BF16),
                   jax.ShapeDtypeStruct((SEQ, B_HEADS * B_VP), BF16)],
        compiler_params=_params(40, 1),
        name="mla_kv_up",
    )(z, z, g.reshape(1, B_KV_LORA), wk, wv, cos_t, sin_t)


def _resident_spec(shape, index_map):
    return pl.BlockSpec(shape, index_map, pipeline_mode=pl.Buffered(1))


def _qk(q, k_ref, k0, tk, k_cols):
    return lax.dot_general(q, k_ref[pl.ds(k0, tk), k_cols], (((1,), (1,)), ((), ())),
                           preferred_element_type=F32)


def _silu(g):
    return g * jax.nn.sigmoid(g)


def _flash_b_kernel(q_ref, k_ref, v_ref, g_ref, o_ref, m_ref, acc_ref, s0_ref, s1_ref, *, tk):
    nk = k_ref.shape[0] // tk
    q = q_ref[...]
    m_ref[...] = jnp.full(m_ref.shape, -jnp.inf, F32)
    acc_ref[...] = jnp.zeros(acc_ref.shape, F32)

    def scores(s_ref, ki):
        s_ref[...] = _qk(q, k_ref, pl.multiple_of(ki * tk, tk), tk, slice(None))

    def consume(s_ref, ki):
        s = s_ref[...]
        m_prev = m_ref[...]
        m_new = jnp.maximum(m_prev, jnp.max(s, axis=1, keepdims=True))
        p = jnp.exp2(s - m_new).astype(BF16)
        acc_ref[...] = jnp.exp2(m_prev - m_new) * acc_ref[...] + jnp.dot(
            p, v_ref[pl.ds(pl.multiple_of(ki * tk, tk), tk), :], preferred_element_type=F32)
        m_ref[...] = m_new

    scores(s0_ref, 0)

    def pair(j, carry):
        scores(s1_ref, 2 * j + 1)
        consume(s0_ref, 2 * j)
        scores(s0_ref, 2 * j + 2)
        consume(s1_ref, 2 * j + 1)
        return carry

    lax.fori_loop(0, nk // 2 - 1, pair, 0)
    scores(s1_ref, nk - 1)
    consume(s0_ref, nk - 2)
    consume(s1_ref, nk - 1)
    acc = acc_ref[...]
    o = acc[:, :B_V] * (1.0 / acc[:, B_V:B_V + 1])
    o_ref[...] = (o * _silu(g_ref[...].astype(F32))).astype(BF16)


def _flash_b(qb, kb, vb, z, tq=512, tk=2048):
    return pl.pallas_call(
        functools.partial(_flash_b_kernel, tk=tk),
        grid=(B_HEADS, SEQ // tq),
        in_specs=[pl.BlockSpec((tq, B_QK), lambda h, i: (i, h)),
                  _resident_spec((SEQ, B_QK), lambda h, i: (0, h)),
                  _resident_spec((SEQ, B_VP), lambda h, i: (0, h)),
                  pl.BlockSpec((tq, B_V), lambda h, i: (i, OFF_GB // B_V + h))],
        out_specs=pl.BlockSpec((tq, B_V), lambda h, i: (i, h)),
        out_shape=jax.ShapeDtypeStruct((SEQ, B_WIDTH), BF16),
        scratch_shapes=[pltpu.VMEM((tq, 1), F32), pltpu.VMEM((tq, B_VP), F32),
                        pltpu.VMEM((tq, tk), F32), pltpu.VMEM((tq, tk), F32)],
        compiler_params=_params(58, 2),
        name="flash_mla",
    )(qb, kb, vb, z)


def _flash_a_kernel(slopes_ref, laminit_ref, q_ref, k_ref, v_ref, g_ref, subg_ref,
                    lamv_ref, o_ref, m_ref, l_ref, acc_ref, s0_ref, s1_ref, *, tq, tk, scale):
    nk = k_ref.shape[0] // tk
    h = pl.program_id(0)
    q0 = pl.program_id(1) * tq
    kd = q0 // tk
    slope2 = slopes_ref[h] * LOG2E
    col = lax.broadcasted_iota(jnp.int32, (1, tk), 1).astype(F32) * slope2
    row = lax.broadcasted_iota(jnp.int32, (tq, 1), 0)

    def chunk(t):
        return t + (t >= kd).astype(jnp.int32)

    outs = []
    for mp in range(2):
        k_cols = slice(mp * HEAD_DIM, (mp + 1) * HEAD_DIM)
        q = (q_ref[:, k_cols].astype(F32) * (scale * LOG2E)).astype(BF16)
        m_ref[...] = jnp.full(m_ref.shape, -jnp.inf, F32)
        l_ref[...] = jnp.zeros(l_ref.shape, F32)
        acc_ref[mp] = jnp.zeros(acc_ref.shape[1:], F32)

        def scores(s_ref, ki, q=q, k_cols=k_cols):
            s_ref[...] = _qk(q, k_ref, pl.multiple_of(ki * tk, tk), tk, k_cols)

        def update(t, c, ki, mp=mp):
            m_prev = m_ref[...]
            m_new = jnp.maximum(m_prev, jnp.max(t, axis=1, keepdims=True) + c)
            alpha = jnp.exp2(m_prev - m_new)
            p = jnp.exp2(t - (m_new - c))
            l_ref[...] = alpha * l_ref[...] + jnp.sum(p, axis=1, keepdims=True)
            acc_ref[mp] = alpha * acc_ref[mp] + jnp.dot(
                p.astype(BF16), v_ref[pl.ds(pl.multiple_of(ki * tk, tk), tk), :],
                preferred_element_type=F32)
            m_ref[...] = m_new

        def consume(s_ref, ki):
            sgn = jnp.where(ki < kd, 1.0, -1.0).astype(F32)
            c = (-sgn * slope2) * (row + (q0 - ki * tk)).astype(F32)
            update(s_ref[...] + sgn * col, c, ki)

        def consume_diag(s_ref):
            rel = row - lax.broadcasted_iota(jnp.int32, (tq, tk), 1) + (q0 - kd * tk)
            update(s_ref[...] - slope2 * jnp.abs(rel).astype(F32), jnp.zeros((tq, 1), F32), kd)

        scores(s0_ref, chunk(0))

        def pair(j, carry):
            scores(s1_ref, chunk(2 * j + 1))
            consume(s0_ref, chunk(2 * j))
            scores(s0_ref, chunk(2 * j + 2))
            consume(s1_ref, chunk(2 * j + 1))
            return carry

        lax.fori_loop(0, (nk - 2) // 2, pair, 0)
        scores(s1_ref, kd)
        consume(s0_ref, chunk(nk - 2))
        consume_diag(s1_ref)
        outs.append(acc_ref[mp] * (1.0 / l_ref[...]))

    lam_init = laminit_ref[0]
    lv = lamv_ref[...]
    lam = (jnp.exp(jnp.sum(lv[0:1] * lv[1:2], axis=1, keepdims=True))
           - jnp.exp(jnp.sum(lv[2:3] * lv[3:4], axis=1, keepdims=True)) + lam_init)
    o = outs[0] - lam * outs[1]
    o = _rms(o, subg_ref[...]) * (1.0 - lam_init)
    o_ref[...] = (o * _silu(g_ref[...].astype(F32))).astype(BF16)


def _flash_a(z, slopes, lam_init, subln_g, lam_vecs, tq=512, tk=2048):
    w = 2 * HEAD_DIM
    smem = pl.BlockSpec(memory_space=pltpu.SMEM)
    return pl.pallas_call(
        functools.partial(_flash_a_kernel, tq=tq, tk=tk, scale=HEAD_DIM ** -0.5),
        grid=(A_HEADS, SEQ // tq),
        in_specs=[smem, smem,
                  pl.BlockSpec((tq, w), lambda h, i: (i, OFF_QA // w + h)),
                  _resident_spec((SEQ, w), lambda h, i: (0, OFF_KA // w + h)),
                  _resident_spec((SEQ, w), lambda h, i: (0, OFF_VA // w + h)),
                  pl.BlockSpec((tq, w), lambda h, i: (i, OFF_GA // w + h)),
                  pl.BlockSpec((1, w), lambda h, i: (0, 0)),
                  pl.BlockSpec((4, HEAD_DIM), lambda h, i: (0, 0))],
        out_specs=pl.BlockSpec((tq, w), lambda h, i: (i, h)),
        out_shape=jax.ShapeDtypeStruct((SEQ, A_WIDTH), BF16),
        scratch_shapes=[pltpu.VMEM((tq, 1), F32), pltpu.VMEM((tq, 1), F32),
                        pltpu.VMEM((2, tq, w), F32),
                        pltpu.VMEM((tq, tk), F32), pltpu.VMEM((tq, tk), F32)],
        compiler_params=_params(58, 2),
        name="flash_diff",
    )(slopes, lam_init, z, z, z, z, subln_g.reshape(1, w), lam_vecs)


NA_GROUP_ROWS = 8
NA_Q = NA_GROUP_ROWS * GRID_W
NA_KROWS = 2 * NA_GROUP_ROWS
NA_K = NA_KROWS * GRID_W
NA_KBLK = 256
NA_GROUPS = SEQ // NA_Q
N_DR = 2 * NA_ROWS - 1
N_DC = 2 * NA_COLS - 1
NA_VARIANTS = (
    (0, lambda rq: max(rq - NA_ROWS // 2, 0)),
    (-NA_ROWS // 2, lambda rq: rq),
    (-NA_ROWS, lambda rq: min(rq + NA_ROWS // 2, NA_ROWS)),
)


def _na_bias_kernel(rpb_ref, o_ref):
    h = pl.program_id(0)
    lane = lax.broadcasted_iota(jnp.int32, (GRID_W, LANES), 1)
    c = lax.broadcasted_iota(jnp.int32, (GRID_W, LANES), 0)
    cp = lane & (GRID_W - 1)
    cs = jnp.clip(c - NA_COLS // 2, 0, GRID_W - NA_COLS)
    col_valid = (cp >= cs) & (cp < cs + NA_COLS)
    dc = cp - c + (NA_COLS - 1)
    neg = jnp.full((GRID_W, LANES), NEG, F32)
    base = h * (N_DR * N_DC)
    tabs = []
    for dr in range(N_DR):
        def jb(j, acc, dr=dr):
            return jnp.where(dc == j, rpb_ref[base + dr * N_DC + j], acc)
        t = lax.fori_loop(0, N_DC, jb, neg)
        tabs.append(jnp.where(col_valid, t, neg))
    low = lane < GRID_W
    for var, (delta, rs_fn) in enumerate(NA_VARIANTS):
        for rq in range(NA_GROUP_ROWS):
            rs = rs_fn(rq)
            for pair in range(NA_KROWS // 2):
                halves = []
                for rk in (2 * pair, 2 * pair + 1):
                    if rs <= rk < rs + NA_ROWS:
                        halves.append(tabs[rk - rq + delta + NA_ROWS - 1])
                    else:
                        halves.append(neg)
                o_ref[0, var, rq * GRID_W:(rq + 1) * GRID_W,
                      pair * LANES:(pair + 1) * LANES] = jnp.where(low, halves[0], halves[1])


def _na_bias(rpb):
    return pl.pallas_call(
        _na_bias_kernel,
        grid=(C_HEADS,),
        in_specs=[pl.BlockSpec(memory_space=pltpu.SMEM)],
        out_specs=pl.BlockSpec((1, len(NA_VARIANTS), NA_Q, NA_K), lambda h: (h, 0, 0, 0)),
        out_shape=jax.ShapeDtypeStruct((C_HEADS, len(NA_VARIANTS), NA_Q, NA_K), F32),
        compiler_params=_params(32, 1),
        name="na_bias",
    )(rpb.reshape(-1))


def _na_kernel(q_ref, k0_ref, k1_ref, k2_ref, k3_ref, v0_ref, v1_ref, v2_ref, v3_ref,
               b_ref, g_ref, o_ref, *, scale):
    q = (q_ref[...].astype(F32) * scale).astype(BF16)
    s = jnp.concatenate(
        [lax.dot_general(q, kr[...], (((1,), (1,)), ((), ())), preferred_element_type=F32)
         for kr in (k0_ref, k1_ref, k2_ref, k3_ref)], axis=1) + b_ref[0, 0]
    m = jnp.max(s, axis=1, keepdims=True)
    p = jnp.exp(s - m)
    l = jnp.sum(p, axis=1, keepdims=True)
    pb = p.astype(BF16)
    o = None
    for i, vr in enumerate((v0_ref, v1_ref, v2_ref, v3_ref)):
        t = jnp.dot(pb[:, i * NA_KBLK:(i + 1) * NA_KBLK], vr[...], preferred_element_type=F32)
        o = t if o is None else o + t
    o = o * (1.0 / l)
    o_ref[...] = (o * _silu(g_ref[...].astype(F32))).astype(BF16)


def _na(z, bias):
    d = HEAD_DIM
    last_start = (SEQ - NA_K) // NA_KBLK

    def kstart(g):
        return jnp.clip(2 * g - 1, 0, last_start)

    def kv_spec(off, i):
        return pl.BlockSpec((NA_KBLK, d), lambda h, g: (kstart(g) + i, off // d + h))

    def variant(g):
        return jnp.where(g == 0, 0, jnp.where(g == NA_GROUPS - 1, 2, 1))

    return pl.pallas_call(
        functools.partial(_na_kernel, scale=d ** -0.5),
        grid=(C_HEADS, NA_GROUPS),
        in_specs=([pl.BlockSpec((NA_Q, d), lambda h, g: (g, OFF_QC // d + h))]
                  + [kv_spec(OFF_KC, i) for i in range(4)]
                  + [kv_spec(OFF_VC, i) for i in range(4)]
                  + [pl.BlockSpec((1, 1, NA_Q, NA_K), lambda h, g: (h, variant(g), 0, 0)),
                     pl.BlockSpec((NA_Q, d), lambda h, g: (g, OFF_GC // d + h))]),
        out_specs=pl.BlockSpec((NA_Q, d), lambda h, g: (g, h)),
        out_shape=jax.ShapeDtypeStruct((SEQ, C_WIDTH), BF16),
        compiler_params=_params(40, 2),
        name="na_attn",
    )(z, z, z, z, z, z, z, z, z, bias, z)


def _merge_kernel(ya_ref, yb_ref, yc_ref, wa_ref, wb_ref, wc_ref,
                  sa_ref, sb_ref, sc_ref, o_ref):
    def branch(y_ref, w_ref, s_ref):
        return jax.nn.sigmoid(s_ref[...].astype(F32)) * jnp.dot(
            y_ref[...], w_ref[...], preferred_element_type=F32)
    o_ref[...] = (branch(ya_ref, wa_ref, sa_ref) + branch(yb_ref, wb_ref, sb_ref)
                  + branch(yc_ref, wc_ref, sc_ref)).astype(BF16)


def _merge(ya, yb, yc, wa, wb, wc, z, tm=512, tn=512):
    def y_spec(width):
        return pl.BlockSpec((tm, width), lambda i, j: (i, 0))

    def w_spec(width):
        return pl.BlockSpec((width, tn), lambda i, j: (0, j))

    def s_spec(off):
        return pl.BlockSpec((tm, tn), lambda i, j: (i, off // tn + j))

    return pl.pallas_call(
        _merge_kernel,
        grid=(SEQ // tm, D_MODEL // tn),
        in_specs=[y_spec(A_WIDTH), y_spec(B_WIDTH), y_spec(C_WIDTH),
                  w_spec(A_WIDTH), w_spec(B_WIDTH), w_spec(C_WIDTH),
                  s_spec(OFF_SA), s_spec(OFF_SB), s_spec(OFF_SC)],
        out_specs=pl.BlockSpec((tm, tn), lambda i, j: (i, j)),
        out_shape=jax.ShapeDtypeStruct((SEQ, D_MODEL), BF16),
        compiler_params=_params(48, 2),
        name="merge",
    )(ya, yb, yc, wa, wb, wc, z, z, z)


def _swap_halves(w):
    half = w.shape[-1] // 2
    return jnp.concatenate([w[..., half:], w[..., :half]], axis=-1)


def _prep_w_in(w):
    o = np.cumsum((0,) + (A_WIDTH,) * 4 + (B_Q_LORA, B_KV_LORA, B_ROPE, B_WIDTH)
                  + (C_WIDTH,) * 4 + (D_MODEL,) * 3)
    a_all = w[:, o[0]:o[4]]
    cq = w[:, o[4]:o[5]]
    ckv = w[:, o[5]:o[6]]
    kr = w[:, o[6]:o[7]]
    gb = w[:, o[7]:o[8]]
    c_all = w[:, o[8]:o[12]]
    gates = w[:, o[12]:o[15]]
    pad = jnp.zeros((w.shape[0], Z_COLS - Z_USED), w.dtype)
    return jnp.concatenate([cq, ckv, a_all, gates, c_all, gb, kr, _swap_halves(kr), pad],
                           axis=1).astype(BF16)


def _prep_w_uq(w):
    w = w.reshape(B_Q_LORA, B_HEADS, B_NOPE + B_ROPE)
    rope = w[..., B_NOPE:]
    return jnp.concatenate([w[..., :B_NOPE], rope, _swap_halves(rope)],
                           axis=-1).reshape(B_Q_LORA, B_HEADS * B_QK).astype(BF16)


def _prep_w_ukv(w):
    w = w.reshape(B_KV_LORA, B_HEADS, B_NOPE + B_V)
    wk = w[..., :B_NOPE].reshape(B_KV_LORA, B_HEADS * B_NOPE).astype(BF16)
    wv = w[..., B_NOPE:].reshape(B_KV_LORA, B_WIDTH).astype(BF16)
    return wk, wv


def _rope_tables():
    inv_freq = ROPE_BASE ** (-jnp.arange(0, B_ROPE, 2, dtype=F32) / B_ROPE)
    ang = jnp.arange(SEQ, dtype=F32)[:, None] * inv_freq[None, :]
    cos, sin = jnp.cos(ang), jnp.sin(ang)
    zero = jnp.zeros((SEQ, LANES - B_ROPE), F32)
    return (jnp.concatenate([cos, cos, zero], axis=1),
            jnp.concatenate([-sin, sin, zero], axis=1))


def kernel(x, norm_g, w_in, a_lam_q1, a_lam_k1, a_lam_q2, a_lam_k2, a_subln_g,
           b_q_norm_g, b_kv_norm_g, b_w_uq, b_w_ukv, c_rpb, w_br_a, w_br_b, w_br_c,
           w_o, final_norm_g):
    assert x.shape == (1, SEQ, D_MODEL)
    cos_t, sin_t = _rope_tables()
    slopes = jnp.asarray(2.0 ** (-8.0 * np.arange(1, A_HEADS + 1) / A_HEADS), dtype=F32)
    lam_inits = jnp.asarray([0.8 - 0.6 * math.exp(-0.3 * l) for l in range(DEPTH)],
                            dtype=F32).reshape(DEPTH, 1)
    lam_vecs = jnp.stack([a_lam_q1, a_lam_k1, a_lam_q2, a_lam_k2], axis=1).astype(F32)

    def layer(xc, p):
        (g_l, w_in_l, lam_init_l, lam_vecs_l, subg_l, qg_l, kvg_l, wuq_l, wukv_l,
         rpb_l, wa_l, wb_l, wc_l, wo_l) = p
        h = _rmsnorm(xc, g_l, BF16)
        z = _matmul(h, _prep_w_in(w_in_l), BF16, 1024, Z_TN, "in_proj")
        ya = _flash_a(z, slopes, lam_init_l, subg_l, lam_vecs_l)
        qb = _q_up(z, qg_l, _prep_w_uq(wuq_l), cos_t, sin_t)
        wk, wv = _prep_w_ukv(wukv_l)
        kb, vb = _kv_up(z, kvg_l, wk, wv, cos_t, sin_t)
        yb = _flash_b(qb, kb, vb, z)
        yc = _na(z, _na_bias(rpb_l))
        merged = _merge(ya, yb, yc, wa_l.astype(BF16), wb_l.astype(BF16),
                        wc_l.astype(BF16), z)
        return _matmul_residual(merged, wo_l.astype(BF16), xc, 512, 1024), None

    xs = (norm_g, w_in, lam_inits, lam_vecs, a_subln_g, b_q_norm_g, b_kv_norm_g,
          b_w_uq, b_w_ukv, c_rpb, w_br_a, w_br_b, w_br_c, w_o)
    xf, _ = lax.scan(layer, x[0], xs)
    return _rmsnorm(xf, final_norm_g, F32)[None]
```

```python
import functools
import math

import numpy as np
import jax
import jax.numpy as jnp
from jax import lax
from jax.experimental import pallas as pl
from jax.experimental.pallas import tpu as pltpu

F32 = jnp.float32
BF16 = jnp.bfloat16

D_MODEL = 4096
SEQ = 16384
DEPTH = 4
GRID_W = 64
HEAD_DIM = 128
A_HEADS = 8
A_WIDTH = A_HEADS * 2 * HEAD_DIM
B_HEADS = 16
B_Q_LORA = 1536
B_KV_LORA = 512
B_NOPE = 128
B_ROPE = 64
B_V = 128
B_WIDTH = B_HEADS * B_V
C_HEADS = 16
C_WIDTH = C_HEADS * HEAD_DIM
NA_ROWS = 8
NA_COLS = 16
ROPE_BASE = 10000.0
EPS = 1e-6

LANES = 128
MIB = 1024 * 1024
NEG = -1e30

OFF_CQ = 0
OFF_CKV = OFF_CQ + B_Q_LORA
OFF_QA = OFF_CKV + B_KV_LORA
OFF_KA = OFF_QA + A_WIDTH
OFF_VA = OFF_KA + A_WIDTH
OFF_GA = OFF_VA + A_WIDTH
OFF_SA = OFF_GA + A_WIDTH
OFF_SB = OFF_SA + D_MODEL
OFF_SC = OFF_SB + D_MODEL
OFF_QC = OFF_SC + D_MODEL
OFF_KC = OFF_QC + C_WIDTH
OFF_VC = OFF_KC + C_WIDTH
OFF_GC = OFF_VC + C_WIDTH
OFF_GB = OFF_GC + C_WIDTH
OFF_KR = OFF_GB + B_WIDTH
Z_USED = OFF_KR + 2 * B_ROPE
Z_TN = 512
Z_COLS = -(-Z_USED // Z_TN) * Z_TN

B_QK = 2 * LANES
B_VP = 2 * LANES
LOG2E = math.log2(math.e)


def _params(vmem_mib, ndims):
    return pltpu.CompilerParams(dimension_semantics=("arbitrary",) * ndims,
                                vmem_limit_bytes=vmem_mib * MIB)


def _rms(x, g):
    ms = jnp.mean(x * x, axis=-1, keepdims=True)
    return x * lax.rsqrt(ms + EPS) * g


def _rmsnorm_kernel(x_ref, g_ref, o_ref):
    o_ref[...] = _rms(x_ref[...], g_ref[...]).astype(o_ref.dtype)


def _rmsnorm(x, g, out_dtype, tm=256):
    m, d = x.shape
    return pl.pallas_call(
        _rmsnorm_kernel,
        grid=(m // tm,),
        in_specs=[pl.BlockSpec((tm, d), lambda i: (i, 0)),
                  pl.BlockSpec((1, d), lambda i: (0, 0))],
        out_specs=pl.BlockSpec((tm, d), lambda i: (i, 0)),
        out_shape=jax.ShapeDtypeStruct((m, d), out_dtype),
        compiler_params=_params(40, 1),
        name="rmsnorm",
    )(x, g.reshape(1, d))


def _mm_kernel(a_ref, b_ref, o_ref):
    o_ref[...] = jnp.dot(a_ref[...], b_ref[...],
                         preferred_element_type=F32).astype(o_ref.dtype)


def _matmul(a, b, out_dtype, tm, tn, name):
    m, k = a.shape
    n = b.shape[1]
    return pl.pallas_call(
        _mm_kernel,
        grid=(m // tm, n // tn),
        in_specs=[pl.BlockSpec((tm, k), lambda i, j: (i, 0)),
                  pl.BlockSpec((k, tn), lambda i, j: (0, j))],
        out_specs=pl.BlockSpec((tm, tn), lambda i, j: (i, j)),
        out_shape=jax.ShapeDtypeStruct((m, n), out_dtype),
        compiler_params=_params(48, 2),
        name=name,
    )(a, b)


def _mm_res_kernel(a_ref, b_ref, x_ref, o_ref):
    o_ref[...] = x_ref[...] + jnp.dot(a_ref[...], b_ref[...],
                                      preferred_element_type=F32)


def _matmul_residual(a, b, x, tm, tn):
    m, k = a.shape
    n = b.shape[1]
    return pl.pallas_call(
        _mm_res_kernel,
        grid=(m // tm, n // tn),
        in_specs=[pl.BlockSpec((tm, k), lambda i, j: (i, 0)),
                  pl.BlockSpec((k, tn), lambda i, j: (0, j)),
                  pl.BlockSpec((tm, tn), lambda i, j: (i, j))],
        out_specs=pl.BlockSpec((tm, tn), lambda i, j: (i, j)),
        out_shape=jax.ShapeDtypeStruct((m, n), F32),
        compiler_params=_params(48, 2),
        name="out_proj",
    )(a, b, x)


def _rope128(u, cos_t, sin_t):
    return u * cos_t + pltpu.roll(u, 2 * (B_ROPE // 2), 1) * sin_t


def _qup_kernel(cq_ref, g_ref, w_ref, cos_ref, sin_ref, o_ref, *, heads, scale):
    hn = _rms(cq_ref[...].astype(F32), g_ref[...]).astype(BF16)
    acc = jnp.dot(hn, w_ref[...], preferred_element_type=F32)
    cos_t = cos_ref[...]
    sin_t = sin_ref[...]
    for h in range(heads):
        lo = acc[:, h * B_QK:h * B_QK + LANES]
        up = acc[:, h * B_QK + LANES:(h + 1) * B_QK]
        o_ref[:, h * B_QK:h * B_QK + LANES] = (lo * scale).astype(BF16)
        o_ref[:, h * B_QK + LANES:(h + 1) * B_QK] = (
            _rope128(up, cos_t, sin_t) * scale).astype(BF16)


def _q_up(z, g, w, cos_t, sin_t, tm=512, heads=4):
    tn = heads * B_QK
    n = B_HEADS * B_QK
    scale = (B_NOPE + B_ROPE) ** -0.5 * LOG2E
    return pl.pallas_call(
        functools.partial(_qup_kernel, heads=heads, scale=scale),
        grid=(SEQ // tm, n // tn),
        in_specs=[pl.BlockSpec((tm, B_Q_LORA), lambda i, j: (i, OFF_CQ // B_Q_LORA)),
                  pl.BlockSpec((1, B_Q_LORA), lambda i, j: (0, 0)),
                  pl.BlockSpec((B_Q_LORA, tn), lambda i, j: (0, j)),
                  pl.BlockSpec((tm, LANES), lambda i, j: (i, 0)),
                  pl.BlockSpec((tm, LANES), lambda i, j: (i, 0))],
        out_specs=pl.BlockSpec((tm, tn), lambda i, j: (i, j)),
        out_shape=jax.ShapeDtypeStruct((SEQ, n), BF16),
        compiler_params=_params(40, 2),
        name="mla_q_up",
    )(z, g.reshape(1, B_Q_LORA), w, cos_t, sin_t)


def _kvup_kernel(ckv_ref, kr_ref, g_ref, wk_ref, wv_ref, cos_ref, sin_ref,
                 k_ref, v_ref):
    hn = _rms(ckv_ref[...].astype(F32), g_ref[...]).astype(BF16)
    kn = jnp.dot(hn, wk_ref[...], preferred_element_type=F32)
    vn = jnp.dot(hn, wv_ref[...], preferred_element_type=F32)
    kr = _rope128(kr_ref[...].astype(F32), cos_ref[...], sin_ref[...]).astype(BF16)
    lane = lax.broadcasted_iota(jnp.int32, (kn.shape[0], LANES), 1)
    ones_col = jnp.where(lane == 0, 1.0, 0.0).astype(BF16)
    for h in range(B_HEADS):
        k_ref[:, h * B_QK:h * B_QK + LANES] = kn[:, h * B_NOPE:(h + 1) * B_NOPE].astype(BF16)
        k_ref[:, h * B_QK + LANES:(h + 1) * B_QK] = kr
        v_ref[:, h * B_VP:h * B_VP + B_V] = vn[:, h * B_V:(h + 1) * B_V].astype(BF16)
        v_ref[:, h * B_VP + B_V:(h + 1) * B_VP] = ones_col


def _kv_up(z, g, wk, wv, cos_t, sin_t, tm=512):
    return pl.pallas_call(
        _kvup_kernel,
        grid=(SEQ // tm,),
        in_specs=[pl.BlockSpec((tm, B_KV_LORA), lambda i: (i, OFF_CKV // B_KV_LORA)),
                  pl.BlockSpec((tm, LANES), lambda i: (i, OFF_KR // LANES)),
                  pl.BlockSpec((1, B_KV_LORA), lambda i: (0, 0)),
                  pl.BlockSpec((B_KV_LORA, B_HEADS * B_NOPE), lambda i: (0, 0)),
                  pl.BlockSpec((B_KV_LORA, B_WIDTH), lambda i: (0, 0)),
                  pl.BlockSpec((tm, LANES), lambda i: (i, 0)),
                  pl.BlockSpec((tm, LANES), lambda i: (i, 0))],
        out_specs=[pl.BlockSpec((tm, B_HEADS * B_QK), lambda i: (i, 0)),
                   pl.BlockSpec((tm, B_HEADS * B_VP), lambda i: (i, 0))],
        out_shape=[jax.ShapeDtypeStruct((SEQ, B_HEADS * B_QK), BF16),
                   jax.ShapeDtypeStruct((SEQ, B_HEADS * B_VP), BF16)],
        compiler_params=_params(40, 1),
        name="mla_kv_up",
    )(z, z, g.reshape(1, B_KV_LORA), wk, wv, cos_t, sin_t)


def _resident_spec(shape, index_map):
    return pl.BlockSpec(shape, index_map, pipeline_mode=pl.Buffered(1))


def _aligned_ds(start, size):
    if not isinstance(start, int):
        start = pl.multiple_of(start, size)
    return pl.ds(start, size)


FLASH_TK = 2048
SCORE_COLS = 512


def _lane_fold(x, op):
    out = x[:, :LANES]
    for j in range(1, x.shape[1] // LANES):
        out = op(out, x[:, j * LANES:(j + 1) * LANES])
    return out


def _score_stage(q, k_ref, k0, tk, k_cols, bias_fn, s_ref, x_ref):
    xacc = None
    for c0 in range(0, tk, SCORE_COLS):
        s = lax.dot_general(q, k_ref[_aligned_ds(k0 + c0, SCORE_COLS), k_cols],
                            (((1,), (1,)), ((), ())), preferred_element_type=F32)
        if bias_fn is not None:
            s = bias_fn(s, c0)
        s_ref[:, c0:c0 + SCORE_COLS] = s
        fold = _lane_fold(s, jnp.maximum)
        xacc = fold if xacc is None else jnp.maximum(xacc, fold)
    x_ref[...] = jnp.max(xacc, axis=1, keepdims=True)


def _prob_stage(s_ref, p_ref, sub):
    subb = jnp.broadcast_to(sub, (sub.shape[0], LANES))
    lacc = None
    for c0 in range(0, s_ref.shape[1], LANES):
        p = jnp.exp2(s_ref[:, c0:c0 + LANES] - subb)
        p_ref[:, c0:c0 + LANES] = p.astype(BF16)
        lacc = p if lacc is None else lacc + p
    return lacc


def _silu(g):
    return g * jax.nn.sigmoid(g)


def _flash_b_kernel(q_ref, k_ref, v_ref, g_ref, o_ref, m_ref, acc_ref,
                    s0_ref, s1_ref, p0_ref, p1_ref, a0_ref, a1_ref, x0_ref, x1_ref, *, tk, th):
    nk = k_ref.shape[0] // tk
    n_items = (q_ref.shape[0] // th) * nk
    bufs = ((s0_ref, p0_ref, a0_ref, x0_ref), (s1_ref, p1_ref, a1_ref, x1_ref))
    m_ref[...] = jnp.full(m_ref.shape, -jnp.inf, F32)
    acc_ref[...] = jnp.zeros(acc_ref.shape, F32)

    def rows(i):
        return _aligned_ds((i // nk) * th, th)

    def keys(i):
        return _aligned_ds((i % nk) * tk, tk)

    def scores(i, b):
        s_ref, _, _, x_ref = bufs[b]
        _score_stage(q_ref[rows(i), :], k_ref, keys(i).start, tk, slice(None), None,
                     s_ref, x_ref)

    def softmax(i, b):
        s_ref, p_ref, a_ref, x_ref = bufs[b]
        m_prev = m_ref[rows(i), :]
        m_new = jnp.maximum(m_prev, x_ref[...])
        _prob_stage(s_ref, p_ref, m_new)
        a_ref[...] = jnp.exp2(m_prev - m_new)
        m_ref[rows(i), :] = m_new

    def pv(i, b):
        _, p_ref, a_ref, _ = bufs[b]
        acc_ref[rows(i), :] = a_ref[...] * acc_ref[rows(i), :] + jnp.dot(
            p_ref[...], v_ref[keys(i), :], preferred_element_type=F32)

    scores(0, 0)
    scores(1, 1)
    softmax(0, 0)

    def pair(j, carry):
        i = 2 * j
        scores(i + 2, 0)
        softmax(i + 1, 1)
        pv(i, 0)
        scores(i + 3, 1)
        softmax(i + 2, 0)
        pv(i + 1, 1)
        return carry

    lax.fori_loop(0, n_items // 2 - 1, pair, 0)
    softmax(n_items - 1, 1)
    pv(n_items - 2, 0)
    pv(n_items - 1, 1)
    acc = acc_ref[...]
    o = acc[:, :B_V] * (1.0 / acc[:, B_V:B_V + 1])
    o_ref[...] = (o * _silu(g_ref[...].astype(F32))).astype(BF16)


def _flash_b(qb, kb, vb, z, tq=1024, tk=None, th=512):
    tk = tk or FLASH_TK
    return pl.pallas_call(
        functools.partial(_flash_b_kernel, tk=tk, th=th),
        grid=(B_HEADS, SEQ // tq),
        in_specs=[pl.BlockSpec((tq, B_QK), lambda h, i: (i, h)),
                  _resident_spec((SEQ, B_QK), lambda h, i: (0, h)),
                  _resident_spec((SEQ, B_VP), lambda h, i: (0, h)),
                  pl.BlockSpec((tq, B_V), lambda h, i: (i, OFF_GB // B_V + h))],
        out_specs=pl.BlockSpec((tq, B_V), lambda h, i: (i, h)),
        out_shape=jax.ShapeDtypeStruct((SEQ, B_WIDTH), BF16),
        scratch_shapes=[pltpu.VMEM((tq, 1), F32), pltpu.VMEM((tq, B_VP), F32),
                        pltpu.VMEM((th, tk), F32), pltpu.VMEM((th, tk), F32),
                        pltpu.VMEM((th, tk), BF16), pltpu.VMEM((th, tk), BF16)]
                       + [pltpu.VMEM((th, 1), F32)] * 4,
        compiler_params=_params(58, 2),
        name="flash_mla",
    )(qb, kb, vb, z)


def _flash_a_kernel(slopes_ref, laminit_ref, q_ref, k_ref, v_ref, g_ref, subg_ref,
                    lamv_ref, o_ref, m_ref, l_ref, acc_ref, qs_ref, s0_ref, s1_ref, p0_ref,
                    p1_ref, a0_ref, a1_ref, x0_ref, x1_ref, *, tq, tk, scale):
    nk = v_ref.shape[0] // tk
    assert nk >= 4 and nk % 2 == 0
    h = pl.program_id(0)
    q0 = pl.program_id(1) * tq
    kd = q0 // tk
    slope2 = slopes_ref[h] * LOG2E
    col = lax.broadcasted_iota(jnp.int32, (1, tk), 1).astype(F32) * slope2
    row = lax.broadcasted_iota(jnp.int32, (tq, 1), 0)
    bufs = ((s0_ref, p0_ref, a0_ref, x0_ref), (s1_ref, p1_ref, a1_ref, x1_ref))

    for mp in range(2):
        k_cols = slice(mp * HEAD_DIM, (mp + 1) * HEAD_DIM)
        qs_ref[mp] = (q_ref[:, k_cols].astype(F32) * (scale * LOG2E)).astype(BF16)
    m_ref[...] = jnp.full(m_ref.shape, -jnp.inf, F32)
    l_ref[...] = jnp.zeros(l_ref.shape, F32)
    acc_ref[...] = jnp.zeros(acc_ref.shape, F32)

    def one_map(mp, carry):
        k_map = k_ref.at[0, mp]

        def chunk(t):
            if isinstance(t, int) and t == nk - 1:
                return kd, True
            return t + (t >= kd).astype(jnp.int32), False

        def scores(t, b):
            ki, diag = chunk(t)
            s_ref, _, _, x_ref = bufs[b]
            if diag:
                def bias(s, c0):
                    rel = row - lax.broadcasted_iota(jnp.int32, s.shape, 1) + (q0 - ki * tk - c0)
                    return s - slope2 * jnp.abs(rel).astype(F32)
            else:
                sgn_col = jnp.where(ki < kd, 1.0, -1.0).astype(F32) * col

                def bias(s, c0):
                    return s + sgn_col[:, c0:c0 + s.shape[1]]
            _score_stage(qs_ref[mp], k_map, pl.multiple_of(ki * tk, tk), tk, slice(None), bias,
                         s_ref, x_ref)

        def softmax(t, b):
            ki, diag = chunk(t)
            s_ref, p_ref, a_ref, x_ref = bufs[b]
            if diag:
                c = jnp.zeros((tq, 1), F32)
            else:
                sgn = jnp.where(ki < kd, 1.0, -1.0).astype(F32)
                c = (-sgn * slope2) * (row + (q0 - ki * tk)).astype(F32)
            m_prev = m_ref[mp]
            m_new = jnp.maximum(m_prev, x_ref[...] + c)
            alpha = jnp.exp2(m_prev - m_new)
            lacc = _prob_stage(s_ref, p_ref, m_new - c)
            l_ref[mp] = alpha * l_ref[mp] + jnp.sum(lacc, axis=1, keepdims=True)
            a_ref[...] = alpha
            m_ref[mp] = m_new

        def pv(t, b):
            ki, _ = chunk(t)
            _, p_ref, a_ref, _ = bufs[b]
            acc_ref[mp] = a_ref[...] * acc_ref[mp] + jnp.dot(
                p_ref[...], v_ref[pl.ds(pl.multiple_of(ki * tk, tk), tk), :],
                preferred_element_type=F32)

        scores(0, 0)
        scores(1, 1)
        softmax(0, 0)

        def pair(j, c2):
            t = 2 * j
            scores(t + 2, 0)
            softmax(t + 1, 1)
            pv(t, 0)
            scores(t + 3, 1)
            softmax(t + 2, 0)
            pv(t + 1, 1)
            return c2

        lax.fori_loop(0, (nk - 4) // 2, pair, 0)
        scores(nk - 2, 0)
        softmax(nk - 3, 1)
        pv(nk - 4, 0)
        scores(nk - 1, 1)
        softmax(nk - 2, 0)
        pv(nk - 3, 1)
        softmax(nk - 1, 1)
        pv(nk - 2, 0)
        pv(nk - 1, 1)
        return carry

    lax.fori_loop(0, 2, one_map, 0)
    outs = [acc_ref[mp] * (1.0 / l_ref[mp]) for mp in range(2)]

    lam_init = laminit_ref[0]
    lv = lamv_ref[...]
    lam = (jnp.exp(jnp.sum(lv[0:1] * lv[1:2], axis=1, keepdims=True))
           - jnp.exp(jnp.sum(lv[2:3] * lv[3:4], axis=1, keepdims=True)) + lam_init)
    o = outs[0] - lam * outs[1]
    o = _rms(o, subg_ref[...]) * (1.0 - lam_init)
    o_ref[...] = (o * _silu(g_ref[...].astype(F32))).astype(BF16)


def _flash_a(z, slopes, lam_init, subln_g, lam_vecs, tq=512, tk=None):
    tk = tk or FLASH_TK
    w = 2 * HEAD_DIM
    smem = pl.BlockSpec(memory_space=pltpu.SMEM)
    ka = z[:, OFF_KA:OFF_KA + A_WIDTH].reshape(SEQ, A_HEADS, 2, HEAD_DIM).transpose(1, 2, 0, 3)
    return pl.pallas_call(
        functools.partial(_flash_a_kernel, tq=tq, tk=tk, scale=HEAD_DIM ** -0.5),
        grid=(A_HEADS, SEQ // tq),
        in_specs=[smem, smem,
                  pl.BlockSpec((tq, w), lambda h, i: (i, OFF_QA // w + h)),
                  _resident_spec((1, 2, SEQ, HEAD_DIM), lambda h, i: (h, 0, 0, 0)),
                  _resident_spec((SEQ, w), lambda h, i: (0, OFF_VA // w + h)),
                  pl.BlockSpec((tq, w), lambda h, i: (i, OFF_GA // w + h)),
                  pl.BlockSpec((1, w), lambda h, i: (0, 0)),
                  pl.BlockSpec((4, HEAD_DIM), lambda h, i: (0, 0))],
        out_specs=pl.BlockSpec((tq, w), lambda h, i: (i, h)),
        out_shape=jax.ShapeDtypeStruct((SEQ, A_WIDTH), BF16),
        scratch_shapes=[pltpu.VMEM((2, tq, 1), F32), pltpu.VMEM((2, tq, 1), F32),
                        pltpu.VMEM((2, tq, w), F32), pltpu.VMEM((2, tq, HEAD_DIM), BF16),
                        pltpu.VMEM((tq, tk), F32), pltpu.VMEM((tq, tk), F32),
                        pltpu.VMEM((tq, tk), BF16), pltpu.VMEM((tq, tk), BF16)]
                       + [pltpu.VMEM((tq, 1), F32)] * 4,
        compiler_params=_params(58, 2),
        name="flash_diff",
    )(slopes, lam_init, z, ka, z, z, subln_g.reshape(1, w), lam_vecs)


NA_GROUP_ROWS = 8
NA_Q = NA_GROUP_ROWS * GRID_W
NA_KROWS = 2 * NA_GROUP_ROWS
NA_K = NA_KROWS * GRID_W
NA_KBLK = 256
NA_GROUPS = SEQ // NA_Q
N_DR = 2 * NA_ROWS - 1
N_DC = 2 * NA_COLS - 1
NA_VARIANTS = (
    (0, lambda rq: max(rq - NA_ROWS // 2, 0)),
    (-NA_ROWS // 2, lambda rq: rq),
    (-NA_ROWS, lambda rq: min(rq + NA_ROWS // 2, NA_ROWS)),
)


def _na_bias_kernel(rpb_ref, o_ref):
    h = pl.program_id(0)
    lane = lax.broadcasted_iota(jnp.int32, (GRID_W, LANES), 1)
    c = lax.broadcasted_iota(jnp.int32, (GRID_W, LANES), 0)
    cp = lane & (GRID_W - 1)
    cs = jnp.clip(c - NA_COLS // 2, 0, GRID_W - NA_COLS)
    col_valid = (cp >= cs) & (cp < cs + NA_COLS)
    dc = cp - c + (NA_COLS - 1)
    neg = jnp.full((GRID_W, LANES), NEG, F32)
    base = h * (N_DR * N_DC)
    tabs = []
    for dr in range(N_DR):
        def jb(j, acc, dr=dr):
            return jnp.where(dc == j, rpb_ref[base + dr * N_DC + j], acc)
        t = lax.fori_loop(0, N_DC, jb, neg)
        tabs.append(jnp.where(col_valid, t, neg))
    low = lane < GRID_W
    for var, (delta, rs_fn) in enumerate(NA_VARIANTS):
        for rq in range(NA_GROUP_ROWS):
            rs = rs_fn(rq)
            for pair in range(NA_KROWS // 2):
                halves = []
                for rk in (2 * pair, 2 * pair + 1):
                    if rs <= rk < rs + NA_ROWS:
                        halves.append(tabs[rk - rq + delta + NA_ROWS - 1])
                    else:
                        halves.append(neg)
                o_ref[0, var, rq * GRID_W:(rq + 1) * GRID_W,
                      pair * LANES:(pair + 1) * LANES] = jnp.where(low, halves[0], halves[1])


def _na_bias(rpb):
    return pl.pallas_call(
        _na_bias_kernel,
        grid=(C_HEADS,),
        in_specs=[pl.BlockSpec(memory_space=pltpu.SMEM)],
        out_specs=pl.BlockSpec((1, len(NA_VARIANTS), NA_Q, NA_K), lambda h: (h, 0, 0, 0)),
        out_shape=jax.ShapeDtypeStruct((C_HEADS, len(NA_VARIANTS), NA_Q, NA_K), F32),
        compiler_params=_params(32, 1),
        name="na_bias",
    )(rpb.reshape(-1))


def _na_kernel(q_ref, k0_ref, k1_ref, k2_ref, k3_ref, v0_ref, v1_ref, v2_ref, v3_ref,
               b_ref, g_ref, o_ref, *, scale, heads):
    d = HEAD_DIM
    for hh in range(heads):
        cols = slice(hh * d, (hh + 1) * d)
        q = (q_ref[:, cols].astype(F32) * scale).astype(BF16)
        s = jnp.concatenate(
            [lax.dot_general(q, kr[:, cols], (((1,), (1,)), ((), ())),
                             preferred_element_type=F32)
             for kr in (k0_ref, k1_ref, k2_ref, k3_ref)], axis=1) + b_ref[hh, 0]
        m = jnp.max(s, axis=1, keepdims=True)
        p = jnp.exp(s - m)
        l = jnp.sum(p, axis=1, keepdims=True)
        pb = p.astype(BF16)
        o = None
        for i, vr in enumerate((v0_ref, v1_ref, v2_ref, v3_ref)):
            t = jnp.dot(pb[:, i * NA_KBLK:(i + 1) * NA_KBLK], vr[:, cols],
                        preferred_element_type=F32)
            o = t if o is None else o + t
        o = o * (1.0 / l)
        o_ref[:, cols] = (o * _silu(g_ref[:, cols].astype(F32))).astype(BF16)


def _na(z, bias, heads=2):
    w = heads * HEAD_DIM
    last_start = (SEQ - NA_K) // NA_KBLK

    def kstart(g):
        return jnp.clip(2 * g - 1, 0, last_start)

    def kv_spec(off, i):
        return pl.BlockSpec((NA_KBLK, w), lambda h, g: (kstart(g) + i, off // w + h))

    def variant(g):
        return jnp.where(g == 0, 0, jnp.where(g == NA_GROUPS - 1, 2, 1))

    return pl.pallas_call(
        functools.partial(_na_kernel, scale=HEAD_DIM ** -0.5, heads=heads),
        grid=(C_HEADS // heads, NA_GROUPS),
        in_specs=([pl.BlockSpec((NA_Q, w), lambda h, g: (g, OFF_QC // w + h))]
                  + [kv_spec(OFF_KC, i) for i in range(4)]
                  + [kv_spec(OFF_VC, i) for i in range(4)]
                  + [pl.BlockSpec((heads, 1, NA_Q, NA_K), lambda h, g: (h, variant(g), 0, 0)),
                     pl.BlockSpec((NA_Q, w), lambda h, g: (g, OFF_GC // w + h))]),
        out_specs=pl.BlockSpec((NA_Q, w), lambda h, g: (g, h)),
        out_shape=jax.ShapeDtypeStruct((SEQ, C_WIDTH), BF16),
        compiler_params=_params(48, 2),
        name="na_attn",
    )(z, z, z, z, z, z, z, z, z, bias, z)


def _merge_kernel(ya_ref, yb_ref, yc_ref, wa_ref, wb_ref, wc_ref,
                  sa_ref, sb_ref, sc_ref, o_ref):
    def branch(y_ref, w_ref, s_ref):
        return jax.nn.sigmoid(s_ref[...].astype(F32)) * jnp.dot(
            y_ref[...], w_ref[...], preferred_element_type=F32)
    o_ref[...] = (branch(ya_ref, wa_ref, sa_ref) + branch(yb_ref, wb_ref, sb_ref)
                  + branch(yc_ref, wc_ref, sc_ref)).astype(BF16)


def _merge(ya, yb, yc, wa, wb, wc, z, tm=512, tn=512):
    def y_spec(width):
        return pl.BlockSpec((tm, width), lambda i, j: (i, 0))

    def w_spec(width):
        return pl.BlockSpec((width, tn), lambda i, j: (0, j))

    def s_spec(off):
        return pl.BlockSpec((tm, tn), lambda i, j: (i, off // tn + j))

    return pl.pallas_call(
        _merge_kernel,
        grid=(SEQ // tm, D_MODEL // tn),
        in_specs=[y_spec(A_WIDTH), y_spec(B_WIDTH), y_spec(C_WIDTH),
                  w_spec(A_WIDTH), w_spec(B_WIDTH), w_spec(C_WIDTH),
                  s_spec(OFF_SA), s_spec(OFF_SB), s_spec(OFF_SC)],
        out_specs=pl.BlockSpec((tm, tn), lambda i, j: (i, j)),
        out_shape=jax.ShapeDtypeStruct((SEQ, D_MODEL), BF16),
        compiler_params=_params(48, 2),
        name="merge",
    )(ya, yb, yc, wa, wb, wc, z, z, z)


def _swap_halves(w):
    half = w.shape[-1] // 2
    return jnp.concatenate([w[..., half:], w[..., :half]], axis=-1)


def _prep_w_in(w):
    o = np.cumsum((0,) + (A_WIDTH,) * 4 + (B_Q_LORA, B_KV_LORA, B_ROPE, B_WIDTH)
                  + (C_WIDTH,) * 4 + (D_MODEL,) * 3)
    a_all = w[:, o[0]:o[4]]
    cq = w[:, o[4]:o[5]]
    ckv = w[:, o[5]:o[6]]
    kr = w[:, o[6]:o[7]]
    gb = w[:, o[7]:o[8]]
    c_all = w[:, o[8]:o[12]]
    gates = w[:, o[12]:o[15]]
    pad = jnp.zeros((w.shape[0], Z_COLS - Z_USED), w.dtype)
    return jnp.concatenate([cq, ckv, a_all, gates, c_all, gb, kr, _swap_halves(kr), pad],
                           axis=1).astype(BF16)


def _prep_w_uq(w):
    w = w.reshape(B_Q_LORA, B_HEADS, B_NOPE + B_ROPE)
    rope = w[..., B_NOPE:]
    return jnp.concatenate([w[..., :B_NOPE], rope, _swap_halves(rope)],
                           axis=-1).reshape(B_Q_LORA, B_HEADS * B_QK).astype(BF16)


def _prep_w_ukv(w):
    w = w.reshape(B_KV_LORA, B_HEADS, B_NOPE + B_V)
    wk = w[..., :B_NOPE].reshape(B_KV_LORA, B_HEADS * B_NOPE).astype(BF16)
    wv = w[..., B_NOPE:].reshape(B_KV_LORA, B_WIDTH).astype(BF16)
    return wk, wv


def _rope_tables():
    inv_freq = ROPE_BASE ** (-jnp.arange(0, B_ROPE, 2, dtype=F32) / B_ROPE)
    ang = jnp.arange(SEQ, dtype=F32)[:, None] * inv_freq[None, :]
    cos, sin = jnp.cos(ang), jnp.sin(ang)
    zero = jnp.zeros((SEQ, LANES - B_ROPE), F32)
    return (jnp.concatenate([cos, cos, zero], axis=1),
            jnp.concatenate([-sin, sin, zero], axis=1))


def kernel(x, norm_g, w_in, a_lam_q1, a_lam_k1, a_lam_q2, a_lam_k2, a_subln_g,
           b_q_norm_g, b_kv_norm_g, b_w_uq, b_w_ukv, c_rpb, w_br_a, w_br_b, w_br_c,
           w_o, final_norm_g):
    assert x.shape == (1, SEQ, D_MODEL)
    cos_t, sin_t = _rope_tables()
    slopes = jnp.asarray(2.0 ** (-8.0 * np.arange(1, A_HEADS + 1) / A_HEADS), dtype=F32)
    lam_inits = jnp.asarray([0.8 - 0.6 * math.exp(-0.3 * l) for l in range(DEPTH)],
                            dtype=F32).reshape(DEPTH, 1)
    lam_vecs = jnp.stack([a_lam_q1, a_lam_k1, a_lam_q2, a_lam_k2], axis=1).astype(F32)

    def layer(xc, p):
        (g_l, w_in_l, lam_init_l, lam_vecs_l, subg_l, qg_l, kvg_l, wuq_l, wukv_l,
         rpb_l, wa_l, wb_l, wc_l, wo_l) = p
        h = _rmsnorm(xc, g_l, BF16)
        z = _matmul(h, _prep_w_in(w_in_l), BF16, 1024, Z_TN, "in_proj")
        ya = _flash_a(z, slopes, lam_init_l, subg_l, lam_vecs_l)
        qb = _q_up(z, qg_l, _prep_w_uq(wuq_l), cos_t, sin_t)
        wk, wv = _prep_w_ukv(wukv_l)
        kb, vb = _kv_up(z, kvg_l, wk, wv, cos_t, sin_t)
        yb = _flash_b(qb, kb, vb, z)
        yc = _na(z, _na_bias(rpb_l))
        merged = _merge(ya, yb, yc, wa_l.astype(BF16), wb_l.astype(BF16),
                        wc_l.astype(BF16), z)
        return _matmul_residual(merged, wo_l.astype(BF16), xc, 512, 1024), None

    xs = (norm_g, w_in, lam_inits, lam_vecs, a_subln_g, b_q_norm_g, b_kv_norm_g,
          b_w_uq, b_w_ukv, c_rpb, w_br_a, w_br_b, w_br_c, w_o)
    xf, _ = lax.scan(layer, x[0], xs)
    return _rmsnorm(xf, final_norm_g, F32)[None]
```

```python
import functools
import math

import numpy as np
import jax
import jax.numpy as jnp
from jax import lax
from jax.experimental import pallas as pl
from jax.experimental.pallas import tpu as pltpu

F32 = jnp.float32
BF16 = jnp.bfloat16

D_MODEL = 4096
SEQ = 16384
DEPTH = 4
GRID_W = 64
HEAD_DIM = 128
A_HEADS = 8
A_WIDTH = A_HEADS * 2 * HEAD_DIM
B_HEADS = 16
B_Q_LORA = 1536
B_KV_LORA = 512
B_NOPE = 128
B_ROPE = 64
B_V = 128
B_WIDTH = B_HEADS * B_V
C_HEADS = 16
C_WIDTH = C_HEADS * HEAD_DIM
NA_ROWS = 8
NA_COLS = 16
ROPE_BASE = 10000.0
EPS = 1e-6

LANES = 128
MIB = 1024 * 1024
NEG = -1e30

OFF_CQ = 0
OFF_CKV = OFF_CQ + B_Q_LORA
OFF_QA = OFF_CKV + B_KV_LORA
OFF_KA = OFF_QA + A_WIDTH
OFF_VA = OFF_KA + A_WIDTH
OFF_GA = OFF_VA + A_WIDTH
OFF_SA = OFF_GA + A_WIDTH
OFF_SB = OFF_SA + D_MODEL
OFF_SC = OFF_SB + D_MODEL
OFF_QC = OFF_SC + D_MODEL
OFF_KC = OFF_QC + C_WIDTH
OFF_VC = OFF_KC + C_WIDTH
OFF_GC = OFF_VC + C_WIDTH
OFF_GB = OFF_GC + C_WIDTH
OFF_KR = OFF_GB + B_WIDTH
Z_USED = OFF_KR + 2 * B_ROPE
Z_TN = 512
Z_COLS = -(-Z_USED // Z_TN) * Z_TN

B_QK = 2 * LANES
B_VP = 2 * LANES
LOG2E = math.log2(math.e)
FLASH_TK = 2048


def _params(vmem_mib, ndims):
    return pltpu.CompilerParams(dimension_semantics=("arbitrary",) * ndims,
                                vmem_limit_bytes=vmem_mib * MIB)


def _rms(x, g):
    ms = jnp.mean(x * x, axis=-1, keepdims=True)
    return x * lax.rsqrt(ms + EPS) * g


def _rmsnorm_kernel(x_ref, g_ref, o_ref):
    o_ref[...] = _rms(x_ref[...], g_ref[...]).astype(o_ref.dtype)


def _rmsnorm(x, g, out_dtype, tm=256):
    m, d = x.shape
    return pl.pallas_call(
        _rmsnorm_kernel,
        grid=(m // tm,),
        in_specs=[pl.BlockSpec((tm, d), lambda i: (i, 0)),
                  pl.BlockSpec((1, d), lambda i: (0, 0))],
        out_specs=pl.BlockSpec((tm, d), lambda i: (i, 0)),
        out_shape=jax.ShapeDtypeStruct((m, d), out_dtype),
        compiler_params=_params(40, 1),
        name="rmsnorm",
    )(x, g.reshape(1, d))


def _in_proj_kernel(layer_ref, a_ref, b_ref, o_ref):
    del layer_ref
    o_ref[...] = jnp.dot(a_ref[...], b_ref[...],
                         preferred_element_type=F32).astype(o_ref.dtype)


def _in_proj(h, w_all, layer, tm=1024, tn=Z_TN):
    m, k = h.shape
    n = w_all.shape[2]
    return pl.pallas_call(
        _in_proj_kernel,
        grid_spec=pltpu.PrefetchScalarGridSpec(
            num_scalar_prefetch=1,
            grid=(m // tm, n // tn),
            in_specs=[pl.BlockSpec((tm, k), lambda i, j, l: (i, 0)),
                      pl.BlockSpec((None, k, tn), lambda i, j, l: (l[0], 0, j))],
            out_specs=pl.BlockSpec((tm, tn), lambda i, j, l: (i, j))),
        out_shape=jax.ShapeDtypeStruct((m, n), BF16),
        compiler_params=_params(48, 2),
        name="in_proj",
    )(layer, h, w_all)


def _mm_res_kernel(a_ref, b_ref, x_ref, o_ref):
    o_ref[...] = x_ref[...] + jnp.dot(a_ref[...], b_ref[...],
                                      preferred_element_type=F32)


def _matmul_residual(a, b, x, tm, tn):
    m, k = a.shape
    n = b.shape[1]
    return pl.pallas_call(
        _mm_res_kernel,
        grid=(m // tm, n // tn),
        in_specs=[pl.BlockSpec((tm, k), lambda i, j: (i, 0)),
                  pl.BlockSpec((k, tn), lambda i, j: (0, j)),
                  pl.BlockSpec((tm, tn), lambda i, j: (i, j))],
        out_specs=pl.BlockSpec((tm, tn), lambda i, j: (i, j)),
        out_shape=jax.ShapeDtypeStruct((m, n), F32),
        compiler_params=_params(48, 2),
        name="out_proj",
    )(a, b, x)


def _rope128(u, cos_t, sin_t):
    return u * cos_t + pltpu.roll(u, 2 * (B_ROPE // 2), 1) * sin_t


def _qup_kernel(cq_ref, g_ref, w_ref, cos_ref, sin_ref, o_ref, *, heads, scale):
    hn = _rms(cq_ref[...].astype(F32), g_ref[...]).astype(BF16)
    acc = jnp.dot(hn, w_ref[...], preferred_element_type=F32)
    cos_t = cos_ref[...]
    sin_t = sin_ref[...]
    for h in range(heads):
        lo = acc[:, h * B_QK:h * B_QK + LANES]
        up = acc[:, h * B_QK + LANES:(h + 1) * B_QK]
        o_ref[:, h * B_QK:h * B_QK + LANES] = (lo * scale).astype(BF16)
        o_ref[:, h * B_QK + LANES:(h + 1) * B_QK] = (
            _rope128(up, cos_t, sin_t) * scale).astype(BF16)


def _q_up(z, g, w, cos_t, sin_t, tm=512, heads=4):
    tn = heads * B_QK
    n = B_HEADS * B_QK
    scale = (B_NOPE + B_ROPE) ** -0.5 * LOG2E
    return pl.pallas_call(
        functools.partial(_qup_kernel, heads=heads, scale=scale),
        grid=(SEQ // tm, n // tn),
        in_specs=[pl.BlockSpec((tm, B_Q_LORA), lambda i, j: (i, OFF_CQ // B_Q_LORA)),
                  pl.BlockSpec((1, B_Q_LORA), lambda i, j: (0, 0)),
                  pl.BlockSpec((B_Q_LORA, tn), lambda i, j: (0, j)),
                  pl.BlockSpec((tm, LANES), lambda i, j: (i, 0)),
                  pl.BlockSpec((tm, LANES), lambda i, j: (i, 0))],
        out_specs=pl.BlockSpec((tm, tn), lambda i, j: (i, j)),
        out_shape=jax.ShapeDtypeStruct((SEQ, n), BF16),
        compiler_params=_params(40, 2),
        name="mla_q_up",
    )(z, g.reshape(1, B_Q_LORA), w, cos_t, sin_t)


def _kvup_kernel(ckv_ref, kr_ref, g_ref, wk_ref, wv_ref, cos_ref, sin_ref,
                 k_ref, v_ref):
    hn = _rms(ckv_ref[...].astype(F32), g_ref[...]).astype(BF16)
    kn = jnp.dot(hn, wk_ref[...], preferred_element_type=F32)
    vn = jnp.dot(hn, wv_ref[...], preferred_element_type=F32)
    kr = _rope128(kr_ref[...].astype(F32), cos_ref[...], sin_ref[...]).astype(BF16)
    lane = lax.broadcasted_iota(jnp.int32, (kn.shape[0], LANES), 1)
    ones_col = jnp.where(lane == 0, 1.0, 0.0).astype(BF16)
    for h in range(B_HEADS):
        k_ref[:, h * B_QK:h * B_QK + LANES] = kn[:, h * B_NOPE:(h + 1) * B_NOPE].astype(BF16)
        k_ref[:, h * B_QK + LANES:(h + 1) * B_QK] = kr
        v_ref[:, h * B_VP:h * B_VP + B_V] = vn[:, h * B_V:(h + 1) * B_V].astype(BF16)
        v_ref[:, h * B_VP + B_V:(h + 1) * B_VP] = ones_col


def _kv_up(z, g, wk, wv, cos_t, sin_t, tm=512):
    return pl.pallas_call(
        _kvup_kernel,
        grid=(SEQ // tm,),
        in_specs=[pl.BlockSpec((tm, B_KV_LORA), lambda i: (i, OFF_CKV // B_KV_LORA)),
                  pl.BlockSpec((tm, LANES), lambda i: (i, OFF_KR // LANES)),
                  pl.BlockSpec((1, B_KV_LORA), lambda i: (0, 0)),
                  pl.BlockSpec((B_KV_LORA, B_HEADS * B_NOPE), lambda i: (0, 0)),
                  pl.BlockSpec((B_KV_LORA, B_WIDTH), lambda i: (0, 0)),
                  pl.BlockSpec((tm, LANES), lambda i: (i, 0)),
                  pl.BlockSpec((tm, LANES), lambda i: (i, 0))],
        out_specs=[pl.BlockSpec((tm, B_HEADS * B_QK), lambda i: (i, 0)),
                   pl.BlockSpec((tm, B_HEADS * B_VP), lambda i: (i, 0))],
        out_shape=[jax.ShapeDtypeStruct((SEQ, B_HEADS * B_QK), BF16),
                   jax.ShapeDtypeStruct((SEQ, B_HEADS * B_VP), BF16)],
        compiler_params=_params(40, 1),
        name="mla_kv_up",
    )(z, z, g.reshape(1, B_KV_LORA), wk, wv, cos_t, sin_t)


def _resident_spec(shape, index_map):
    return pl.BlockSpec(shape, index_map, pipeline_mode=pl.Buffered(1))


def _qk(q, k_ref, k0, tk, k_cols):
    return lax.dot_general(q, k_ref[pl.ds(k0, tk), k_cols], (((1,), (1,)), ((), ())),
                           preferred_element_type=F32)


def _silu(g):
    return g * jax.nn.sigmoid(g)


def _flash_b_kernel(q_ref, k_ref, v_ref, g_ref, o_ref, m_ref, acc_ref, s0_ref, s1_ref, *, tk):
    nk = k_ref.shape[0] // tk
    q = q_ref[...]
    m_ref[...] = jnp.full(m_ref.shape, -jnp.inf, F32)
    acc_ref[...] = jnp.zeros(acc_ref.shape, F32)

    def scores(s_ref, ki):
        s_ref[...] = _qk(q, k_ref, pl.multiple_of(ki * tk, tk), tk, slice(None))

    def consume(s_ref, ki):
        s = s_ref[...]
        m_prev = m_ref[...]
        m_new = jnp.maximum(m_prev, jnp.max(s, axis=1, keepdims=True))
        p = jnp.exp2(s - m_new).astype(BF16)
        acc_ref[...] = jnp.exp2(m_prev - m_new) * acc_ref[...] + jnp.dot(
            p, v_ref[pl.ds(pl.multiple_of(ki * tk, tk), tk), :], preferred_element_type=F32)
        m_ref[...] = m_new

    scores(s0_ref, 0)

    def pair(j, carry):
        scores(s1_ref, 2 * j + 1)
        consume(s0_ref, 2 * j)
        scores(s0_ref, 2 * j + 2)
        consume(s1_ref, 2 * j + 1)
        return carry

    lax.fori_loop(0, nk // 2 - 1, pair, 0)
    scores(s1_ref, nk - 1)
    consume(s0_ref, nk - 2)
    consume(s1_ref, nk - 1)
    acc = acc_ref[...]
    o = acc[:, :B_V] * (1.0 / acc[:, B_V:B_V + 1])
    o_ref[...] = (o * _silu(g_ref[...].astype(F32))).astype(BF16)


def _flash_b(qb, kb, vb, z, tq=512, tk=None):
    tk = tk or FLASH_TK
    return pl.pallas_call(
        functools.partial(_flash_b_kernel, tk=tk),
        grid=(B_HEADS, SEQ // tq),
        in_specs=[pl.BlockSpec((tq, B_QK), lambda h, i: (i, h)),
                  _resident_spec((SEQ, B_QK), lambda h, i: (0, h)),
                  _resident_spec((SEQ, B_VP), lambda h, i: (0, h)),
                  pl.BlockSpec((tq, B_V), lambda h, i: (i, OFF_GB // B_V + h))],
        out_specs=pl.BlockSpec((tq, B_V), lambda h, i: (i, h)),
        out_shape=jax.ShapeDtypeStruct((SEQ, B_WIDTH), BF16),
        scratch_shapes=[pltpu.VMEM((tq, 1), F32), pltpu.VMEM((tq, B_VP), F32),
                        pltpu.VMEM((tq, tk), F32), pltpu.VMEM((tq, tk), F32)],
        compiler_params=_params(58, 2),
        name="flash_mla",
    )(qb, kb, vb, z)


def _flash_a_kernel(slopes_ref, laminit_ref, q_ref, k_ref, v_ref, g_ref, subg_ref,
                    lamv_ref, o_ref, m_ref, l_ref, acc_ref, s0_ref, s1_ref, *, tq, tk, scale):
    nk = k_ref.shape[0] // tk
    h = pl.program_id(0)
    q0 = pl.program_id(1) * tq
    kd = q0 // tk
    slope2 = slopes_ref[h] * LOG2E
    col = lax.broadcasted_iota(jnp.int32, (1, tk), 1).astype(F32) * slope2
    row = lax.broadcasted_iota(jnp.int32, (tq, 1), 0)

    def chunk(t):
        return t + (t >= kd).astype(jnp.int32)

    outs = []
    for mp in range(2):
        k_cols = slice(mp * HEAD_DIM, (mp + 1) * HEAD_DIM)
        q = (q_ref[:, k_cols].astype(F32) * (scale * LOG2E)).astype(BF16)
        m_ref[...] = jnp.full(m_ref.shape, -jnp.inf, F32)
        l_ref[...] = jnp.zeros(l_ref.shape, F32)
        acc_ref[mp] = jnp.zeros(acc_ref.shape[1:], F32)

        def scores(s_ref, ki, q=q, k_cols=k_cols):
            s_ref[...] = _qk(q, k_ref, pl.multiple_of(ki * tk, tk), tk, k_cols)

        def update(t, c, ki, mp=mp):
            m_prev = m_ref[...]
            m_new = jnp.maximum(m_prev, jnp.max(t, axis=1, keepdims=True) + c)
            alpha = jnp.exp2(m_prev - m_new)
            p = jnp.exp2(t - (m_new - c))
            l_ref[...] = alpha * l_ref[...] + jnp.sum(p, axis=1, keepdims=True)
            acc_ref[mp] = alpha * acc_ref[mp] + jnp.dot(
                p.astype(BF16), v_ref[pl.ds(pl.multiple_of(ki * tk, tk), tk), :],
                preferred_element_type=F32)
            m_ref[...] = m_new

        def consume(s_ref, ki):
            sgn = jnp.where(ki < kd, 1.0, -1.0).astype(F32)
            c = (-sgn * slope2) * (row + (q0 - ki * tk)).astype(F32)
            update(s_ref[...] + sgn * col, c, ki)

        def consume_diag(s_ref):
            rel = row - lax.broadcasted_iota(jnp.int32, (tq, tk), 1) + (q0 - kd * tk)
            update(s_ref[...] - slope2 * jnp.abs(rel).astype(F32), jnp.zeros((tq, 1), F32), kd)

        scores(s0_ref, chunk(0))

        def pair(j, carry):
            scores(s1_ref, chunk(2 * j + 1))
            consume(s0_ref, chunk(2 * j))
            scores(s0_ref, chunk(2 * j + 2))
            consume(s1_ref, chunk(2 * j + 1))
            return carry

        lax.fori_loop(0, (nk - 2) // 2, pair, 0)
        scores(s1_ref, kd)
        consume(s0_ref, chunk(nk - 2))
        consume_diag(s1_ref)
        outs.append(acc_ref[mp] * (1.0 / l_ref[...]))

    lam_init = laminit_ref[0]
    lv = lamv_ref[...]
    lam = (jnp.exp(jnp.sum(lv[0:1] * lv[1:2], axis=1, keepdims=True))
           - jnp.exp(jnp.sum(lv[2:3] * lv[3:4], axis=1, keepdims=True)) + lam_init)
    o = outs[0] - lam * outs[1]
    o = _rms(o, subg_ref[...]) * (1.0 - lam_init)
    o_ref[...] = (o * _silu(g_ref[...].astype(F32))).astype(BF16)


def _flash_a(z, slopes, lam_init, subln_g, lam_vecs, tq=512, tk=None):
    tk = tk or FLASH_TK
    w = 2 * HEAD_DIM
    smem = pl.BlockSpec(memory_space=pltpu.SMEM)
    return pl.pallas_call(
        functools.partial(_flash_a_kernel, tq=tq, tk=tk, scale=HEAD_DIM ** -0.5),
        grid=(A_HEADS, SEQ // tq),
        in_specs=[smem, smem,
                  pl.BlockSpec((tq, w), lambda h, i: (i, OFF_QA // w + h)),
                  _resident_spec((SEQ, w), lambda h, i: (0, OFF_KA // w + h)),
                  _resident_spec((SEQ, w), lambda h, i: (0, OFF_VA // w + h)),
                  pl.BlockSpec((tq, w), lambda h, i: (i, OFF_GA // w + h)),
                  pl.BlockSpec((1, w), lambda h, i: (0, 0)),
                  pl.BlockSpec((4, HEAD_DIM), lambda h, i: (0, 0))],
        out_specs=pl.BlockSpec((tq, w), lambda h, i: (i, h)),
        out_shape=jax.ShapeDtypeStruct((SEQ, A_WIDTH), BF16),
        scratch_shapes=[pltpu.VMEM((tq, 1), F32), pltpu.VMEM((tq, 1), F32),
                        pltpu.VMEM((2, tq, w), F32),
                        pltpu.VMEM((tq, tk), F32), pltpu.VMEM((tq, tk), F32)],
        compiler_params=_params(58, 2),
        name="flash_diff",
    )(slopes, lam_init, z, z, z, z, subln_g.reshape(1, w), lam_vecs)


NA_GROUP_ROWS = 8
NA_Q = NA_GROUP_ROWS * GRID_W
NA_KROWS = 2 * NA_GROUP_ROWS
NA_K = NA_KROWS * GRID_W
NA_KBLK = 256
NA_GROUPS = SEQ // NA_Q
N_DR = 2 * NA_ROWS - 1
N_DC = 2 * NA_COLS - 1
NA_VARIANTS = (
    (0, lambda rq: max(rq - NA_ROWS // 2, 0)),
    (-NA_ROWS // 2, lambda rq: rq),
    (-NA_ROWS, lambda rq: min(rq + NA_ROWS // 2, NA_ROWS)),
)


def _na_bias_kernel(rpb_ref, o_ref):
    h = pl.program_id(0)
    lane = lax.broadcasted_iota(jnp.int32, (GRID_W, LANES), 1)
    c = lax.broadcasted_iota(jnp.int32, (GRID_W, LANES), 0)
    cp = lane & (GRID_W - 1)
    cs = jnp.clip(c - NA_COLS // 2, 0, GRID_W - NA_COLS)
    col_valid = (cp >= cs) & (cp < cs + NA_COLS)
    dc = cp - c + (NA_COLS - 1)
    neg = jnp.full((GRID_W, LANES), NEG, F32)
    base = h * (N_DR * N_DC)
    tabs = []
    for dr in range(N_DR):
        def jb(j, acc, dr=dr):
            return jnp.where(dc == j, rpb_ref[base + dr * N_DC + j], acc)
        t = lax.fori_loop(0, N_DC, jb, neg)
        tabs.append(jnp.where(col_valid, t, neg))
    low = lane < GRID_W
    for var, (delta, rs_fn) in enumerate(NA_VARIANTS):
        for rq in range(NA_GROUP_ROWS):
            rs = rs_fn(rq)
            for pair in range(NA_KROWS // 2):
                halves = []
                for rk in (2 * pair, 2 * pair + 1):
                    if rs <= rk < rs + NA_ROWS:
                        halves.append(tabs[rk - rq + delta + NA_ROWS - 1])
                    else:
                        halves.append(neg)
                o_ref[0, var, rq * GRID_W:(rq + 1) * GRID_W,
                      pair * LANES:(pair + 1) * LANES] = jnp.where(low, halves[0], halves[1])


def _na_bias(rpb):
    return pl.pallas_call(
        _na_bias_kernel,
        grid=(C_HEADS,),
        in_specs=[pl.BlockSpec(memory_space=pltpu.SMEM)],
        out_specs=pl.BlockSpec((1, len(NA_VARIANTS), NA_Q, NA_K), lambda h: (h, 0, 0, 0)),
        out_shape=jax.ShapeDtypeStruct((C_HEADS, len(NA_VARIANTS), NA_Q, NA_K), F32),
        compiler_params=_params(32, 1),
        name="na_bias",
    )(rpb.reshape(-1))


def _na_kernel(q_ref, k0_ref, k1_ref, k2_ref, k3_ref, v0_ref, v1_ref, v2_ref, v3_ref,
               b_ref, g_ref, o_ref, *, scale, heads):
    d = HEAD_DIM
    for hh in range(heads):
        cols = slice(hh * d, (hh + 1) * d)
        q = (q_ref[:, cols].astype(F32) * scale).astype(BF16)
        s = jnp.concatenate(
            [lax.dot_general(q, kr[:, cols], (((1,), (1,)), ((), ())),
                             preferred_element_type=F32)
             for kr in (k0_ref, k1_ref, k2_ref, k3_ref)], axis=1) + b_ref[hh, 0]
        m = jnp.max(s, axis=1, keepdims=True)
        p = jnp.exp(s - m)
        l = jnp.sum(p, axis=1, keepdims=True)
        pb = p.astype(BF16)
        o = None
        for i, vr in enumerate((v0_ref, v1_ref, v2_ref, v3_ref)):
            t = jnp.dot(pb[:, i * NA_KBLK:(i + 1) * NA_KBLK], vr[:, cols],
                        preferred_element_type=F32)
            o = t if o is None else o + t
        o = o * (1.0 / l)
        o_ref[:, cols] = (o * _silu(g_ref[:, cols].astype(F32))).astype(BF16)


def _na(z, bias, heads=2):
    w = heads * HEAD_DIM
    last_start = (SEQ - NA_K) // NA_KBLK

    def kstart(g):
        return jnp.clip(2 * g - 1, 0, last_start)

    def kv_spec(off, i):
        return pl.BlockSpec((NA_KBLK, w), lambda h, g: (kstart(g) + i, off // w + h))

    def variant(g):
        return jnp.where(g == 0, 0, jnp.where(g == NA_GROUPS - 1, 2, 1))

    return pl.pallas_call(
        functools.partial(_na_kernel, scale=HEAD_DIM ** -0.5, heads=heads),
        grid=(C_HEADS // heads, NA_GROUPS),
        in_specs=([pl.BlockSpec((NA_Q, w), lambda h, g: (g, OFF_QC // w + h))]
                  + [kv_spec(OFF_KC, i) for i in range(4)]
                  + [kv_spec(OFF_VC, i) for i in range(4)]
                  + [pl.BlockSpec((heads, 1, NA_Q, NA_K), lambda h, g: (h, variant(g), 0, 0)),
                     pl.BlockSpec((NA_Q, w), lambda h, g: (g, OFF_GC // w + h))]),
        out_specs=pl.BlockSpec((NA_Q, w), lambda h, g: (g, h)),
        out_shape=jax.ShapeDtypeStruct((SEQ, C_WIDTH), BF16),
        compiler_params=_params(48, 2),
        name="na_attn",
    )(z, z, z, z, z, z, z, z, z, bias, z)


def _merge_kernel(ya_ref, yb_ref, yc_ref, wa_ref, wb_ref, wc_ref,
                  sa_ref, sb_ref, sc_ref, o_ref):
    def branch(y_ref, w_ref, s_ref):
        return jax.nn.sigmoid(s_ref[...].astype(F32)) * jnp.dot(
            y_ref[...], w_ref[...], preferred_element_type=F32)
    o_ref[...] = (branch(ya_ref, wa_ref, sa_ref) + branch(yb_ref, wb_ref, sb_ref)
                  + branch(yc_ref, wc_ref, sc_ref)).astype(BF16)


def _merge(ya, yb, yc, wa, wb, wc, z, tm=512, tn=512):
    def y_spec(width):
        return pl.BlockSpec((tm, width), lambda i, j: (i, 0))

    def w_spec(width):
        return pl.BlockSpec((width, tn), lambda i, j: (0, j))

    def s_spec(off):
        return pl.BlockSpec((tm, tn), lambda i, j: (i, off // tn + j))

    return pl.pallas_call(
        _merge_kernel,
        grid=(SEQ // tm, D_MODEL // tn),
        in_specs=[y_spec(A_WIDTH), y_spec(B_WIDTH), y_spec(C_WIDTH),
                  w_spec(A_WIDTH), w_spec(B_WIDTH), w_spec(C_WIDTH),
                  s_spec(OFF_SA), s_spec(OFF_SB), s_spec(OFF_SC)],
        out_specs=pl.BlockSpec((tm, tn), lambda i, j: (i, j)),
        out_shape=jax.ShapeDtypeStruct((SEQ, D_MODEL), BF16),
        compiler_params=_params(48, 2),
        name="merge",
    )(ya, yb, yc, wa, wb, wc, z, z, z)


def _swap_halves(w):
    half = w.shape[-1] // 2
    return jnp.concatenate([w[..., half:], w[..., :half]], axis=-1)


def _prep_w_in(w):
    o = np.cumsum((0,) + (A_WIDTH,) * 4 + (B_Q_LORA, B_KV_LORA, B_ROPE, B_WIDTH)
                  + (C_WIDTH,) * 4 + (D_MODEL,) * 3)
    a_all = w[..., o[0]:o[4]]
    cq = w[..., o[4]:o[5]]
    ckv = w[..., o[5]:o[6]]
    kr = w[..., o[6]:o[7]]
    gb = w[..., o[7]:o[8]]
    c_all = w[..., o[8]:o[12]]
    gates = w[..., o[12]:o[15]]
    pad = jnp.zeros(w.shape[:-1] + (Z_COLS - Z_USED,), w.dtype)
    return jnp.concatenate([cq, ckv, a_all, gates, c_all, gb, kr, _swap_halves(kr), pad],
                           axis=-1).astype(BF16)


def _prep_w_uq(w):
    w = w.reshape(w.shape[:-1] + (B_HEADS, B_NOPE + B_ROPE))
    rope = w[..., B_NOPE:]
    w = jnp.concatenate([w[..., :B_NOPE], rope, _swap_halves(rope)], axis=-1)
    return w.reshape(w.shape[:-2] + (B_HEADS * B_QK,)).astype(BF16)


def _prep_w_ukv(w):
    w = w.reshape(w.shape[:-1] + (B_HEADS, B_NOPE + B_V))
    wk = w[..., :B_NOPE].reshape(w.shape[:-2] + (B_HEADS * B_NOPE,)).astype(BF16)
    wv = w[..., B_NOPE:].reshape(w.shape[:-2] + (B_WIDTH,)).astype(BF16)
    return wk, wv


def _rope_tables():
    inv_freq = ROPE_BASE ** (-jnp.arange(0, B_ROPE, 2, dtype=F32) / B_ROPE)
    ang = jnp.arange(SEQ, dtype=F32)[:, None] * inv_freq[None, :]
    cos, sin = jnp.cos(ang), jnp.sin(ang)
    zero = jnp.zeros((SEQ, LANES - B_ROPE), F32)
    return (jnp.concatenate([cos, cos, zero], axis=1),
            jnp.concatenate([-sin, sin, zero], axis=1))


def kernel(x, norm_g, w_in, a_lam_q1, a_lam_k1, a_lam_q2, a_lam_k2, a_subln_g,
           b_q_norm_g, b_kv_norm_g, b_w_uq, b_w_ukv, c_rpb, w_br_a, w_br_b, w_br_c,
           w_o, final_norm_g):
    assert x.shape == (1, SEQ, D_MODEL)
    cos_t, sin_t = _rope_tables()
    slopes = jnp.asarray(2.0 ** (-8.0 * np.arange(1, A_HEADS + 1) / A_HEADS), dtype=F32)
    lam_inits = jnp.asarray([0.8 - 0.6 * math.exp(-0.3 * l) for l in range(DEPTH)],
                            dtype=F32).reshape(DEPTH, 1)
    lam_vecs = jnp.stack([a_lam_q1, a_lam_k1, a_lam_q2, a_lam_k2], axis=1).astype(F32)
    layers = jnp.arange(DEPTH, dtype=jnp.int32).reshape(DEPTH, 1)
    w_in_all = _prep_w_in(w_in)
    wk_all, wv_all = _prep_w_ukv(b_w_ukv)

    def layer(xc, p):
        (l_idx, g_l, lam_init_l, lam_vecs_l, subg_l, qg_l, kvg_l, wuq_l, wk_l, wv_l,
         rpb_l, wa_l, wb_l, wc_l, wo_l) = p
        h = _rmsnorm(xc, g_l, BF16)
        z = _in_proj(h, w_in_all, l_idx)
        ya = _flash_a(z, slopes, lam_init_l, subg_l, lam_vecs_l)
        qb = _q_up(z, qg_l, wuq_l, cos_t, sin_t)
        kb, vb = _kv_up(z, kvg_l, wk_l, wv_l, cos_t, sin_t)
        yb = _flash_b(qb, kb, vb, z)
        yc = _na(z, _na_bias(rpb_l))
        merged = _merge(ya, yb, yc, wa_l, wb_l, wc_l, z)
        return _matmul_residual(merged, wo_l, xc, 512, 1024), None

    xs = (layers, norm_g, lam_inits, lam_vecs, a_subln_g, b_q_norm_g, b_kv_norm_g,
          _prep_w_uq(b_w_uq), wk_all, wv_all, c_rpb, w_br_a.astype(BF16),
          w_br_b.astype(BF16), w_br_c.astype(BF16), w_o.astype(BF16))
    xf, _ = lax.scan(layer, x[0], xs)
    return _rmsnorm(xf, final_norm_g, F32)[None]
```

```python
import functools
import math

import numpy as np
import jax
import jax.numpy as jnp
from jax import lax
from jax.experimental import pallas as pl
from jax.experimental.pallas import tpu as pltpu

F32 = jnp.float32
BF16 = jnp.bfloat16

D_MODEL = 4096
SEQ = 16384
DEPTH = 4
GRID_W = 64
HEAD_DIM = 128
A_HEADS = 8
A_WIDTH = A_HEADS * 2 * HEAD_DIM
B_HEADS = 16
B_Q_LORA = 1536
B_KV_LORA = 512
B_NOPE = 128
B_ROPE = 64
B_V = 128
B_WIDTH = B_HEADS * B_V
C_HEADS = 16
C_WIDTH = C_HEADS * HEAD_DIM
NA_ROWS = 8
NA_COLS = 16
ROPE_BASE = 10000.0
EPS = 1e-6

LANES = 128
MIB = 1024 * 1024
NEG = -1e30

OFF_CQ = 0
OFF_CKV = OFF_CQ + B_Q_LORA
OFF_QA = OFF_CKV + B_KV_LORA
OFF_KA = OFF_QA + A_WIDTH
OFF_VA = OFF_KA + A_WIDTH
OFF_GA = OFF_VA + A_WIDTH
OFF_SA = OFF_GA + A_WIDTH
OFF_SB = OFF_SA + D_MODEL
OFF_SC = OFF_SB + D_MODEL
OFF_QC = OFF_SC + D_MODEL
OFF_KC = OFF_QC + C_WIDTH
OFF_VC = OFF_KC + C_WIDTH
OFF_GC = OFF_VC + C_WIDTH
OFF_GB = OFF_GC + C_WIDTH
OFF_KR = OFF_GB + B_WIDTH
Z_USED = OFF_KR + 2 * B_ROPE
Z_TN = 512
Z_COLS = -(-Z_USED // Z_TN) * Z_TN

B_QK = 2 * LANES
B_VP = 2 * LANES
LOG2E = math.log2(math.e)
FLASH_TK = 2048


def _params(vmem_mib, ndims):
    return pltpu.CompilerParams(dimension_semantics=("arbitrary",) * ndims,
                                vmem_limit_bytes=vmem_mib * MIB)


def _rms(x, g):
    ms = jnp.mean(x * x, axis=-1, keepdims=True)
    return x * lax.rsqrt(ms + EPS) * g


def _rmsnorm_kernel(x_ref, g_ref, o_ref):
    o_ref[...] = _rms(x_ref[...], g_ref[...]).astype(o_ref.dtype)


def _rmsnorm(x, g, out_dtype, tm=256):
    m, d = x.shape
    return pl.pallas_call(
        _rmsnorm_kernel,
        grid=(m // tm,),
        in_specs=[pl.BlockSpec((tm, d), lambda i: (i, 0)),
                  pl.BlockSpec((1, d), lambda i: (0, 0))],
        out_specs=pl.BlockSpec((tm, d), lambda i: (i, 0)),
        out_shape=jax.ShapeDtypeStruct((m, d), out_dtype),
        compiler_params=_params(40, 1),
        name="rmsnorm",
    )(x, g.reshape(1, d))


def _in_proj_kernel(layer_ref, a_ref, b_ref, o_ref):
    del layer_ref
    o_ref[...] = jnp.dot(a_ref[...], b_ref[...],
                         preferred_element_type=F32).astype(o_ref.dtype)


def _in_proj(h, w_all, layer, tm=1024, tn=Z_TN):
    m, k = h.shape
    n = w_all.shape[2]
    return pl.pallas_call(
        _in_proj_kernel,
        grid_spec=pltpu.PrefetchScalarGridSpec(
            num_scalar_prefetch=1,
            grid=(m // tm, n // tn),
            in_specs=[pl.BlockSpec((tm, k), lambda i, j, l: (i, 0)),
                      pl.BlockSpec((None, k, tn), lambda i, j, l: (l[0], 0, j))],
            out_specs=pl.BlockSpec((tm, tn), lambda i, j, l: (i, j))),
        out_shape=jax.ShapeDtypeStruct((m, n), BF16),
        compiler_params=_params(48, 2),
        name="in_proj",
    )(layer, h, w_all)


def _mm_res_kernel(a_ref, b_ref, x_ref, o_ref):
    o_ref[...] = x_ref[...] + jnp.dot(a_ref[...], b_ref[...],
                                      preferred_element_type=F32)


def _matmul_residual(a, b, x, tm, tn):
    m, k = a.shape
    n = b.shape[1]
    return pl.pallas_call(
        _mm_res_kernel,
        grid=(m // tm, n // tn),
        in_specs=[pl.BlockSpec((tm, k), lambda i, j: (i, 0)),
                  pl.BlockSpec((k, tn), lambda i, j: (0, j)),
                  pl.BlockSpec((tm, tn), lambda i, j: (i, j))],
        out_specs=pl.BlockSpec((tm, tn), lambda i, j: (i, j)),
        out_shape=jax.ShapeDtypeStruct((m, n), F32),
        compiler_params=_params(48, 2),
        name="out_proj",
    )(a, b, x)


def _rope128(u, cos_t, sin_t):
    return u * cos_t + pltpu.roll(u, 2 * (B_ROPE // 2), 1) * sin_t


def _qup_kernel(cq_ref, g_ref, w_ref, cos_ref, sin_ref, o_ref, *, heads, scale):
    hn = _rms(cq_ref[...].astype(F32), g_ref[...]).astype(BF16)
    acc = jnp.dot(hn, w_ref[...], preferred_element_type=F32)
    cos_t = cos_ref[...]
    sin_t = sin_ref[...]
    for h in range(heads):
        lo = acc[:, h * B_QK:h * B_QK + LANES]
        up = acc[:, h * B_QK + LANES:(h + 1) * B_QK]
        o_ref[:, h * B_QK:h * B_QK + LANES] = (lo * scale).astype(BF16)
        o_ref[:, h * B_QK + LANES:(h + 1) * B_QK] = (
            _rope128(up, cos_t, sin_t) * scale).astype(BF16)


def _q_up(z, g, w, cos_t, sin_t, tm=512, heads=4):
    tn = heads * B_QK
    n = B_HEADS * B_QK
    scale = (B_NOPE + B_ROPE) ** -0.5 * LOG2E
    return pl.pallas_call(
        functools.partial(_qup_kernel, heads=heads, scale=scale),
        grid=(SEQ // tm, n // tn),
        in_specs=[pl.BlockSpec((tm, B_Q_LORA), lambda i, j: (i, OFF_CQ // B_Q_LORA)),
                  pl.BlockSpec((1, B_Q_LORA), lambda i, j: (0, 0)),
                  pl.BlockSpec((B_Q_LORA, tn), lambda i, j: (0, j)),
                  pl.BlockSpec((tm, LANES), lambda i, j: (i, 0)),
                  pl.BlockSpec((tm, LANES), lambda i, j: (i, 0))],
        out_specs=pl.BlockSpec((tm, tn), lambda i, j: (i, j)),
        out_shape=jax.ShapeDtypeStruct((SEQ, n), BF16),
        compiler_params=_params(40, 2),
        name="mla_q_up",
    )(z, g.reshape(1, B_Q_LORA), w, cos_t, sin_t)


def _kvup_kernel(ckv_ref, kr_ref, g_ref, wk_ref, wv_ref, cos_ref, sin_ref,
                 k_ref, v_ref):
    hn = _rms(ckv_ref[...].astype(F32), g_ref[...]).astype(BF16)
    kn = jnp.dot(hn, wk_ref[...], preferred_element_type=F32)
    vn = jnp.dot(hn, wv_ref[...], preferred_element_type=F32)
    kr = _rope128(kr_ref[...].astype(F32), cos_ref[...], sin_ref[...]).astype(BF16)
    lane = lax.broadcasted_iota(jnp.int32, (kn.shape[0], LANES), 1)
    ones_col = jnp.where(lane == 0, 1.0, 0.0).astype(BF16)
    for h in range(B_HEADS):
        k_ref[:, h * B_QK:h * B_QK + LANES] = kn[:, h * B_NOPE:(h + 1) * B_NOPE].astype(BF16)
        k_ref[:, h * B_QK + LANES:(h + 1) * B_QK] = kr
        v_ref[:, h * B_VP:h * B_VP + B_V] = vn[:, h * B_V:(h + 1) * B_V].astype(BF16)
        v_ref[:, h * B_VP + B_V:(h + 1) * B_VP] = ones_col


def _kv_up(z, g, wk, wv, cos_t, sin_t, tm=512):
    return pl.pallas_call(
        _kvup_kernel,
        grid=(SEQ // tm,),
        in_specs=[pl.BlockSpec((tm, B_KV_LORA), lambda i: (i, OFF_CKV // B_KV_LORA)),
                  pl.BlockSpec((tm, LANES), lambda i: (i, OFF_KR // LANES)),
                  pl.BlockSpec((1, B_KV_LORA), lambda i: (0, 0)),
                  pl.BlockSpec((B_KV_LORA, B_HEADS * B_NOPE), lambda i: (0, 0)),
                  pl.BlockSpec((B_KV_LORA, B_WIDTH), lambda i: (0, 0)),
                  pl.BlockSpec((tm, LANES), lambda i: (i, 0)),
                  pl.BlockSpec((tm, LANES), lambda i: (i, 0))],
        out_specs=[pl.BlockSpec((tm, B_HEADS * B_QK), lambda i: (i, 0)),
                   pl.BlockSpec((tm, B_HEADS * B_VP), lambda i: (i, 0))],
        out_shape=[jax.ShapeDtypeStruct((SEQ, B_HEADS * B_QK), BF16),
                   jax.ShapeDtypeStruct((SEQ, B_HEADS * B_VP), BF16)],
        compiler_params=_params(40, 1),
        name="mla_kv_up",
    )(z, z, g.reshape(1, B_KV_LORA), wk, wv, cos_t, sin_t)


def _resident_spec(shape, index_map):
    return pl.BlockSpec(shape, index_map, pipeline_mode=pl.Buffered(1))


def _qk(q, k_ref, k0, tk, k_cols):
    return lax.dot_general(q, k_ref[pl.ds(k0, tk), k_cols], (((1,), (1,)), ((), ())),
                           preferred_element_type=F32)


def _silu(g):
    return g * jax.nn.sigmoid(g)


def _flash_b_kernel(q_ref, k_ref, v_ref, g_ref, o_ref, m_ref, acc_ref, s0_ref, s1_ref,
                    x0_ref, x1_ref, *, tk):
    nk = k_ref.shape[0] // tk
    q = q_ref[...]
    m_ref[...] = jnp.full(m_ref.shape, -jnp.inf, F32)
    acc_ref[...] = jnp.zeros(acc_ref.shape, F32)
    bufs = ((s0_ref, x0_ref), (s1_ref, x1_ref))

    def scores(b, ki):
        s_ref, x_ref = bufs[b]
        s = _qk(q, k_ref, pl.multiple_of(ki * tk, tk), tk, slice(None))
        s_ref[...] = s
        x_ref[...] = jnp.max(s, axis=1, keepdims=True)

    def consume(b, ki):
        s_ref, x_ref = bufs[b]
        m_prev = m_ref[...]
        m_new = jnp.maximum(m_prev, x_ref[...])
        p = jnp.exp2(s_ref[...] - m_new).astype(BF16)
        acc_ref[...] = jnp.exp2(m_prev - m_new) * acc_ref[...] + jnp.dot(
            p, v_ref[pl.ds(pl.multiple_of(ki * tk, tk), tk), :], preferred_element_type=F32)
        m_ref[...] = m_new

    scores(0, 0)

    def pair(j, carry):
        scores(1, 2 * j + 1)
        consume(0, 2 * j)
        scores(0, 2 * j + 2)
        consume(1, 2 * j + 1)
        return carry

    lax.fori_loop(0, nk // 2 - 1, pair, 0, unroll=True)
    scores(1, nk - 1)
    consume(0, nk - 2)
    consume(1, nk - 1)
    acc = acc_ref[...]
    o = acc[:, :B_V] * (1.0 / acc[:, B_V:B_V + 1])
    o_ref[...] = (o * _silu(g_ref[...].astype(F32))).astype(BF16)


def _flash_b(qb, kb, vb, z, tq=512, tk=None):
    tk = tk or FLASH_TK
    return pl.pallas_call(
        functools.partial(_flash_b_kernel, tk=tk),
        grid=(B_HEADS, SEQ // tq),
        in_specs=[pl.BlockSpec((tq, B_QK), lambda h, i: (i, h)),
                  _resident_spec((SEQ, B_QK), lambda h, i: (0, h)),
                  _resident_spec((SEQ, B_VP), lambda h, i: (0, h)),
                  pl.BlockSpec((tq, B_V), lambda h, i: (i, OFF_GB // B_V + h))],
        out_specs=pl.BlockSpec((tq, B_V), lambda h, i: (i, h)),
        out_shape=jax.ShapeDtypeStruct((SEQ, B_WIDTH), BF16),
        scratch_shapes=[pltpu.VMEM((tq, 1), F32), pltpu.VMEM((tq, B_VP), F32),
                        pltpu.VMEM((tq, tk), F32), pltpu.VMEM((tq, tk), F32),
                        pltpu.VMEM((tq, 1), F32), pltpu.VMEM((tq, 1), F32)],
        compiler_params=_params(58, 2),
        name="flash_mla",
    )(qb, kb, vb, z)


def _flash_a_kernel(slopes_ref, laminit_ref, q_ref, k_ref, v_ref, g_ref, subg_ref,
                    lamv_ref, o_ref, m_ref, l_ref, acc_ref, qs_ref, s0_ref, s1_ref,
                    *, tq, tk, scale):
    nk = v_ref.shape[0] // tk
    h = pl.program_id(0)
    q0 = pl.program_id(1) * tq
    kd = q0 // tk
    slope2 = slopes_ref[h] * LOG2E
    col = lax.broadcasted_iota(jnp.int32, (1, tk), 1).astype(F32) * slope2
    row = lax.broadcasted_iota(jnp.int32, (tq, 1), 0)

    def chunk(t):
        return t + (t >= kd).astype(jnp.int32)

    for mp in range(2):
        k_cols = slice(mp * HEAD_DIM, (mp + 1) * HEAD_DIM)
        qs_ref[mp] = (q_ref[:, k_cols].astype(F32) * (scale * LOG2E)).astype(BF16)

    def one_map(mp, outer):
        k_map = k_ref.at[0, mp]
        q = qs_ref[mp]
        m_ref[...] = jnp.full(m_ref.shape, -jnp.inf, F32)
        l_ref[...] = jnp.zeros(l_ref.shape, F32)
        acc_ref[mp] = jnp.zeros(acc_ref.shape[1:], F32)

        def scores(s_ref, ki):
            s_ref[...] = _qk(q, k_map, pl.multiple_of(ki * tk, tk), tk, slice(None))

        def update(t, c, ki):
            m_prev = m_ref[...]
            m_new = jnp.maximum(m_prev, jnp.max(t, axis=1, keepdims=True) + c)
            alpha = jnp.exp2(m_prev - m_new)
            p = jnp.exp2(t - (m_new - c))
            l_ref[...] = alpha * l_ref[...] + jnp.sum(p, axis=1, keepdims=True)
            acc_ref[mp] = alpha * acc_ref[mp] + jnp.dot(
                p.astype(BF16), v_ref[pl.ds(pl.multiple_of(ki * tk, tk), tk), :],
                preferred_element_type=F32)
            m_ref[...] = m_new

        def consume(s_ref, ki):
            sgn = jnp.where(ki < kd, 1.0, -1.0).astype(F32)
            c = (-sgn * slope2) * (row + (q0 - ki * tk)).astype(F32)
            update(s_ref[...] + sgn * col, c, ki)

        def consume_diag(s_ref):
            rel = row - lax.broadcasted_iota(jnp.int32, (tq, tk), 1) + (q0 - kd * tk)
            update(s_ref[...] - slope2 * jnp.abs(rel).astype(F32), jnp.zeros((tq, 1), F32), kd)

        scores(s0_ref, chunk(0))

        def pair(j, carry):
            scores(s1_ref, chunk(2 * j + 1))
            consume(s0_ref, chunk(2 * j))
            scores(s0_ref, chunk(2 * j + 2))
            consume(s1_ref, chunk(2 * j + 1))
            return carry

        lax.fori_loop(0, (nk - 2) // 2, pair, 0, unroll=True)
        scores(s1_ref, kd)
        consume(s0_ref, chunk(nk - 2))
        consume_diag(s1_ref)
        acc_ref[mp] = acc_ref[mp] * (1.0 / l_ref[...])
        return outer

    lax.fori_loop(0, 2, one_map, 0)

    lam_init = laminit_ref[0]
    lv = lamv_ref[...]
    lam = (jnp.exp(jnp.sum(lv[0:1] * lv[1:2], axis=1, keepdims=True))
           - jnp.exp(jnp.sum(lv[2:3] * lv[3:4], axis=1, keepdims=True)) + lam_init)
    o = acc_ref[0] - lam * acc_ref[1]
    o = _rms(o, subg_ref[...]) * (1.0 - lam_init)
    o_ref[...] = (o * _silu(g_ref[...].astype(F32))).astype(BF16)


def _flash_a(z, slopes, lam_init, subln_g, lam_vecs, tq=512, tk=None):
    tk = tk or FLASH_TK
    w = 2 * HEAD_DIM
    smem = pl.BlockSpec(memory_space=pltpu.SMEM)
    ka = z[:, OFF_KA:OFF_KA + A_WIDTH].reshape(SEQ, A_HEADS, 2, HEAD_DIM).transpose(1, 2, 0, 3)
    return pl.pallas_call(
        functools.partial(_flash_a_kernel, tq=tq, tk=tk, scale=HEAD_DIM ** -0.5),
        grid=(A_HEADS, SEQ // tq),
        in_specs=[smem, smem,
                  pl.BlockSpec((tq, w), lambda h, i: (i, OFF_QA // w + h)),
                  _resident_spec((1, 2, SEQ, HEAD_DIM), lambda h, i: (h, 0, 0, 0)),
                  _resident_spec((SEQ, w), lambda h, i: (0, OFF_VA // w + h)),
                  pl.BlockSpec((tq, w), lambda h, i: (i, OFF_GA // w + h)),
                  pl.BlockSpec((1, w), lambda h, i: (0, 0)),
                  pl.BlockSpec((4, HEAD_DIM), lambda h, i: (0, 0))],
        out_specs=pl.BlockSpec((tq, w), lambda h, i: (i, h)),
        out_shape=jax.ShapeDtypeStruct((SEQ, A_WIDTH), BF16),
        scratch_shapes=[pltpu.VMEM((tq, 1), F32), pltpu.VMEM((tq, 1), F32),
                        pltpu.VMEM((2, tq, w), F32), pltpu.VMEM((2, tq, HEAD_DIM), BF16),
                        pltpu.VMEM((tq, tk), F32), pltpu.VMEM((tq, tk), F32)],
        compiler_params=_params(58, 2),
        name="flash_diff",
    )(slopes, lam_init, z, ka, z, z, subln_g.reshape(1, w), lam_vecs)


NA_GROUP_ROWS = 8
NA_Q = NA_GROUP_ROWS * GRID_W
NA_KROWS = 2 * NA_GROUP_ROWS
NA_K = NA_KROWS * GRID_W
NA_KBLK = 256
NA_GROUPS = SEQ // NA_Q
N_DR = 2 * NA_ROWS - 1
N_DC = 2 * NA_COLS - 1
NA_VARIANTS = (
    (0, lambda rq: max(rq - NA_ROWS // 2, 0)),
    (-NA_ROWS // 2, lambda rq: rq),
    (-NA_ROWS, lambda rq: min(rq + NA_ROWS // 2, NA_ROWS)),
)


def _na_bias_kernel(rpb_ref, o_ref):
    h = pl.program_id(0)
    lane = lax.broadcasted_iota(jnp.int32, (GRID_W, LANES), 1)
    c = lax.broadcasted_iota(jnp.int32, (GRID_W, LANES), 0)
    cp = lane & (GRID_W - 1)
    cs = jnp.clip(c - NA_COLS // 2, 0, GRID_W - NA_COLS)
    col_valid = (cp >= cs) & (cp < cs + NA_COLS)
    dc = cp - c + (NA_COLS - 1)
    neg = jnp.full((GRID_W, LANES), NEG, F32)
    base = h * (N_DR * N_DC)
    tabs = []
    for dr in range(N_DR):
        def jb(j, acc, dr=dr):
            return jnp.where(dc == j, rpb_ref[base + dr * N_DC + j], acc)
        t = lax.fori_loop(0, N_DC, jb, neg)
        tabs.append(jnp.where(col_valid, t, neg))
    low = lane < GRID_W
    for var, (delta, rs_fn) in enumerate(NA_VARIANTS):
        for rq in range(NA_GROUP_ROWS):
            rs = rs_fn(rq)
            for pair in range(NA_KROWS // 2):
                halves = []
                for rk in (2 * pair, 2 * pair + 1):
                    if rs <= rk < rs + NA_ROWS:
                        halves.append(tabs[rk - rq + delta + NA_ROWS - 1])
                    else:
                        halves.append(neg)
                o_ref[0, var, rq * GRID_W:(rq + 1) * GRID_W,
                      pair * LANES:(pair + 1) * LANES] = jnp.where(low, halves[0], halves[1])


def _na_bias(rpb):
    return pl.pallas_call(
        _na_bias_kernel,
        grid=(C_HEADS,),
        in_specs=[pl.BlockSpec(memory_space=pltpu.SMEM)],
        out_specs=pl.BlockSpec((1, len(NA_VARIANTS), NA_Q, NA_K), lambda h: (h, 0, 0, 0)),
        out_shape=jax.ShapeDtypeStruct((C_HEADS, len(NA_VARIANTS), NA_Q, NA_K), F32),
        compiler_params=_params(32, 1),
        name="na_bias",
    )(rpb.reshape(-1))


def _na_kernel(q_ref, k0_ref, k1_ref, k2_ref, k3_ref, v0_ref, v1_ref, v2_ref, v3_ref,
               b_ref, g_ref, o_ref, *, scale, heads):
    d = HEAD_DIM
    for hh in range(heads):
        cols = slice(hh * d, (hh + 1) * d)
        q = (q_ref[:, cols].astype(F32) * scale).astype(BF16)
        s = jnp.concatenate(
            [lax.dot_general(q, kr[:, cols], (((1,), (1,)), ((), ())),
                             preferred_element_type=F32)
             for kr in (k0_ref, k1_ref, k2_ref, k3_ref)], axis=1) + b_ref[hh, 0]
        m = jnp.max(s, axis=1, keepdims=True)
        p = jnp.exp(s - m)
        l = jnp.sum(p, axis=1, keepdims=True)
        pb = p.astype(BF16)
        o = None
        for i, vr in enumerate((v0_ref, v1_ref, v2_ref, v3_ref)):
            t = jnp.dot(pb[:, i * NA_KBLK:(i + 1) * NA_KBLK], vr[:, cols],
                        preferred_element_type=F32)
            o = t if o is None else o + t
        o = o * (1.0 / l)
        o_ref[:, cols] = (o * _silu(g_ref[:, cols].astype(F32))).astype(BF16)


def _na(z, bias, heads=2):
    w = heads * HEAD_DIM
    last_start = (SEQ - NA_K) // NA_KBLK

    def kstart(g):
        return jnp.clip(2 * g - 1, 0, last_start)

    def kv_spec(off, i):
        return pl.BlockSpec((NA_KBLK, w), lambda h, g: (kstart(g) + i, off // w + h))

    def variant(g):
        return jnp.where(g == 0, 0, jnp.where(g == NA_GROUPS - 1, 2, 1))

    return pl.pallas_call(
        functools.partial(_na_kernel, scale=HEAD_DIM ** -0.5, heads=heads),
        grid=(C_HEADS // heads, NA_GROUPS),
        in_specs=([pl.BlockSpec((NA_Q, w), lambda h, g: (g, OFF_QC // w + h))]
                  + [kv_spec(OFF_KC, i) for i in range(4)]
                  + [kv_spec(OFF_VC, i) for i in range(4)]
                  + [pl.BlockSpec((heads, 1, NA_Q, NA_K), lambda h, g: (h, variant(g), 0, 0)),
                     pl.BlockSpec((NA_Q, w), lambda h, g: (g, OFF_GC // w + h))]),
        out_specs=pl.BlockSpec((NA_Q, w), lambda h, g: (g, h)),
        out_shape=jax.ShapeDtypeStruct((SEQ, C_WIDTH), BF16),
        compiler_params=_params(48, 2),
        name="na_attn",
    )(z, z, z, z, z, z, z, z, z, bias, z)


def _merge_kernel(ya_ref, yb_ref, yc_ref, wa_ref, wb_ref, wc_ref,
                  sa_ref, sb_ref, sc_ref, o_ref):
    def branch(y_ref, w_ref, s_ref):
        return jax.nn.sigmoid(s_ref[...].astype(F32)) * jnp.dot(
            y_ref[...], w_ref[...], preferred_element_type=F32)
    o_ref[...] = (branch(ya_ref, wa_ref, sa_ref) + branch(yb_ref, wb_ref, sb_ref)
                  + branch(yc_ref, wc_ref, sc_ref)).astype(BF16)


def _merge(ya, yb, yc, wa, wb, wc, z, tm=512, tn=512):
    def y_spec(width):
        return pl.BlockSpec((tm, width), lambda i, j: (i, 0))

    def w_spec(width):
        return pl.BlockSpec((width, tn), lambda i, j: (0, j))

    def s_spec(off):
        return pl.BlockSpec((tm, tn), lambda i, j: (i, off // tn + j))

    return pl.pallas_call(
        _merge_kernel,
        grid=(SEQ // tm, D_MODEL // tn),
        in_specs=[y_spec(A_WIDTH), y_spec(B_WIDTH), y_spec(C_WIDTH),
                  w_spec(A_WIDTH), w_spec(B_WIDTH), w_spec(C_WIDTH),
                  s_spec(OFF_SA), s_spec(OFF_SB), s_spec(OFF_SC)],
        out_specs=pl.BlockSpec((tm, tn), lambda i, j: (i, j)),
        out_shape=jax.ShapeDtypeStruct((SEQ, D_MODEL), BF16),
        compiler_params=_params(48, 2),
        name="merge",
    )(ya, yb, yc, wa, wb, wc, z, z, z)


def _swap_halves(w):
    half = w.shape[-1] // 2
    return jnp.concatenate([w[..., half:], w[..., :half]], axis=-1)


def _prep_w_in(w):
    o = np.cumsum((0,) + (A_WIDTH,) * 4 + (B_Q_LORA, B_KV_LORA, B_ROPE, B_WIDTH)
                  + (C_WIDTH,) * 4 + (D_MODEL,) * 3)
    a_all = w[..., o[0]:o[4]]
    cq = w[..., o[4]:o[5]]
    ckv = w[..., o[5]:o[6]]
    kr = w[..., o[6]:o[7]]
    gb = w[..., o[7]:o[8]]
    c_all = w[..., o[8]:o[12]]
    gates = w[..., o[12]:o[15]]
    pad = jnp.zeros(w.shape[:-1] + (Z_COLS - Z_USED,), w.dtype)
    return jnp.concatenate([cq, ckv, a_all, gates, c_all, gb, kr, _swap_halves(kr), pad],
                           axis=-1).astype(BF16)


def _prep_w_uq(w):
    w = w.reshape(w.shape[:-1] + (B_HEADS, B_NOPE + B_ROPE))
    rope = w[..., B_NOPE:]
    w = jnp.concatenate([w[..., :B_NOPE], rope, _swap_halves(rope)], axis=-1)
    return w.reshape(w.shape[:-2] + (B_HEADS * B_QK,)).astype(BF16)


def _prep_w_ukv(w):
    w = w.reshape(w.shape[:-1] + (B_HEADS, B_NOPE + B_V))
    wk = w[..., :B_NOPE].reshape(w.shape[:-2] + (B_HEADS * B_NOPE,)).astype(BF16)
    wv = w[..., B_NOPE:].reshape(w.shape[:-2] + (B_WIDTH,)).astype(BF16)
    return wk, wv


def _rope_tables():
    inv_freq = ROPE_BASE ** (-jnp.arange(0, B_ROPE, 2, dtype=F32) / B_ROPE)
    ang = jnp.arange(SEQ, dtype=F32)[:, None] * inv_freq[None, :]
    cos, sin = jnp.cos(ang), jnp.sin(ang)
    zero = jnp.zeros((SEQ, LANES - B_ROPE), F32)
    return (jnp.concatenate([cos, cos, zero], axis=1),
            jnp.concatenate([-sin, sin, zero], axis=1))


def kernel(x, norm_g, w_in, a_lam_q1, a_lam_k1, a_lam_q2, a_lam_k2, a_subln_g,
           b_q_norm_g, b_kv_norm_g, b_w_uq, b_w_ukv, c_rpb, w_br_a, w_br_b, w_br_c,
           w_o, final_norm_g):
    assert x.shape == (1, SEQ, D_MODEL)
    cos_t, sin_t = _rope_tables()
    slopes = jnp.asarray(2.0 ** (-8.0 * np.arange(1, A_HEADS + 1) / A_HEADS), dtype=F32)
    lam_inits = jnp.asarray([0.8 - 0.6 * math.exp(-0.3 * l) for l in range(DEPTH)],
                            dtype=F32).reshape(DEPTH, 1)
    lam_vecs = jnp.stack([a_lam_q1, a_lam_k1, a_lam_q2, a_lam_k2], axis=1).astype(F32)
    layers = jnp.arange(DEPTH, dtype=jnp.int32).reshape(DEPTH, 1)
    w_in_all = _prep_w_in(w_in)
    wk_all, wv_all = _prep_w_ukv(b_w_ukv)

    def layer(xc, p):
        (l_idx, g_l, lam_init_l, lam_vecs_l, subg_l, qg_l, kvg_l, wuq_l, wk_l, wv_l,
         rpb_l, wa_l, wb_l, wc_l, wo_l) = p
        h = _rmsnorm(xc, g_l, BF16)
        z = _in_proj(h, w_in_all, l_idx)
        ya = _flash_a(z, slopes, lam_init_l, subg_l, lam_vecs_l)
        qb = _q_up(z, qg_l, wuq_l, cos_t, sin_t)
        kb, vb = _kv_up(z, kvg_l, wk_l, wv_l, cos_t, sin_t)
        yb = _flash_b(qb, kb, vb, z)
        yc = _na(z, _na_bias(rpb_l))
        merged = _merge(ya, yb, yc, wa_l, wb_l, wc_l, z)
        return _matmul_residual(merged, wo_l, xc, 512, 1024), None

    xs = (layers, norm_g, lam_inits, lam_vecs, a_subln_g, b_q_norm_g, b_kv_norm_g,
          _prep_w_uq(b_w_uq), wk_all, wv_all, c_rpb, w_br_a.astype(BF16),
          w_br_b.astype(BF16), w_br_c.astype(BF16), w_o.astype(BF16))
    xf, _ = lax.scan(layer, x[0], xs)
    return _rmsnorm(xf, final_norm_g, F32)[None]
```

```python
import functools
import math

import numpy as np
import jax
import jax.numpy as jnp
from jax import lax
from jax.experimental import pallas as pl
from jax.experimental.pallas import tpu as pltpu

F32 = jnp.float32
BF16 = jnp.bfloat16

D_MODEL = 4096
SEQ = 16384
DEPTH = 4
GRID_W = 64
HEAD_DIM = 128
A_HEADS = 8
A_WIDTH = A_HEADS * 2 * HEAD_DIM
B_HEADS = 16
B_Q_LORA = 1536
B_KV_LORA = 512
B_NOPE = 128
B_ROPE = 64
B_V = 128
B_WIDTH = B_HEADS * B_V
C_HEADS = 16
C_WIDTH = C_HEADS * HEAD_DIM
NA_ROWS = 8
NA_COLS = 16
ROPE_BASE = 10000.0
EPS = 1e-6

LANES = 128
MIB = 1024 * 1024
NEG = -1e30

OFF_CQ = 0
OFF_CKV = OFF_CQ + B_Q_LORA
OFF_QA = OFF_CKV + B_KV_LORA
OFF_KA = OFF_QA + A_WIDTH
OFF_VA = OFF_KA + A_WIDTH
OFF_GA = OFF_VA + A_WIDTH
OFF_SA = OFF_GA + A_WIDTH
OFF_SB = OFF_SA + D_MODEL
OFF_SC = OFF_SB + D_MODEL
OFF_QC = OFF_SC + D_MODEL
OFF_KC = OFF_QC + C_WIDTH
OFF_VC = OFF_KC + C_WIDTH
OFF_GC = OFF_VC + C_WIDTH
OFF_GB = OFF_GC + C_WIDTH
OFF_KR = OFF_GB + B_WIDTH
Z_USED = OFF_KR + 2 * B_ROPE
Z_TN = 512
Z_COLS = -(-Z_USED // Z_TN) * Z_TN

B_QK = 2 * LANES
B_VP = 2 * LANES
LOG2E = math.log2(math.e)
FLASH_TK = 2048


def _params(vmem_mib, ndims):
    return pltpu.CompilerParams(dimension_semantics=("arbitrary",) * ndims,
                                vmem_limit_bytes=vmem_mib * MIB)


def _rms(x, g):
    ms = jnp.mean(x * x, axis=-1, keepdims=True)
    return x * lax.rsqrt(ms + EPS) * g


def _rmsnorm_kernel(x_ref, g_ref, o_ref):
    o_ref[...] = _rms(x_ref[...], g_ref[...]).astype(o_ref.dtype)


def _rmsnorm(x, g, out_dtype, tm=256):
    m, d = x.shape
    return pl.pallas_call(
        _rmsnorm_kernel,
        grid=(m // tm,),
        in_specs=[pl.BlockSpec((tm, d), lambda i: (i, 0)),
                  pl.BlockSpec((1, d), lambda i: (0, 0))],
        out_specs=pl.BlockSpec((tm, d), lambda i: (i, 0)),
        out_shape=jax.ShapeDtypeStruct((m, d), out_dtype),
        compiler_params=_params(40, 1),
        name="rmsnorm",
    )(x, g.reshape(1, d))


def _in_proj_kernel(layer_ref, a_ref, b_ref, o_ref):
    del layer_ref
    o_ref[...] = jnp.dot(a_ref[...], b_ref[...],
                         preferred_element_type=F32).astype(o_ref.dtype)


def _in_proj(h, w_all, layer, tm=1024, tn=Z_TN):
    m, k = h.shape
    n = w_all.shape[2]
    return pl.pallas_call(
        _in_proj_kernel,
        grid_spec=pltpu.PrefetchScalarGridSpec(
            num_scalar_prefetch=1,
            grid=(m // tm, n // tn),
            in_specs=[pl.BlockSpec((tm, k), lambda i, j, l: (i, 0)),
                      pl.BlockSpec((None, k, tn), lambda i, j, l: (l[0], 0, j))],
            out_specs=pl.BlockSpec((tm, tn), lambda i, j, l: (i, j))),
        out_shape=jax.ShapeDtypeStruct((m, n), BF16),
        compiler_params=_params(48, 2),
        name="in_proj",
    )(layer, h, w_all)


def _mm_res_kernel(a_ref, b_ref, x_ref, o_ref):
    o_ref[...] = x_ref[...] + jnp.dot(a_ref[...], b_ref[...],
                                      preferred_element_type=F32)


def _matmul_residual(a, b, x, tm, tn):
    m, k = a.shape
    n = b.shape[1]
    return pl.pallas_call(
        _mm_res_kernel,
        grid=(m // tm, n // tn),
        in_specs=[pl.BlockSpec((tm, k), lambda i, j: (i, 0)),
                  pl.BlockSpec((k, tn), lambda i, j: (0, j)),
                  pl.BlockSpec((tm, tn), lambda i, j: (i, j))],
        out_specs=pl.BlockSpec((tm, tn), lambda i, j: (i, j)),
        out_shape=jax.ShapeDtypeStruct((m, n), F32),
        compiler_params=_params(48, 2),
        name="out_proj",
    )(a, b, x)


def _rope128(u, cos_t, sin_t):
    return u * cos_t + pltpu.roll(u, 2 * (B_ROPE // 2), 1) * sin_t


def _qup_kernel(cq_ref, g_ref, w_ref, cos_ref, sin_ref, o_ref, *, heads, scale):
    hn = _rms(cq_ref[...].astype(F32), g_ref[...]).astype(BF16)
    acc = jnp.dot(hn, w_ref[...], preferred_element_type=F32)
    cos_t = cos_ref[...]
    sin_t = sin_ref[...]
    for h in range(heads):
        lo = acc[:, h * B_QK:h * B_QK + LANES]
        up = acc[:, h * B_QK + LANES:(h + 1) * B_QK]
        o_ref[:, h * B_QK:h * B_QK + LANES] = (lo * scale).astype(BF16)
        o_ref[:, h * B_QK + LANES:(h + 1) * B_QK] = (
            _rope128(up, cos_t, sin_t) * scale).astype(BF16)


def _q_up(z, g, w, cos_t, sin_t, tm=512, heads=B_HEADS):
    tn = heads * B_QK
    n = B_HEADS * B_QK
    scale = (B_NOPE + B_ROPE) ** -0.5 * LOG2E
    return pl.pallas_call(
        functools.partial(_qup_kernel, heads=heads, scale=scale),
        grid=(SEQ // tm, n // tn),
        in_specs=[pl.BlockSpec((tm, B_Q_LORA), lambda i, j: (i, OFF_CQ // B_Q_LORA)),
                  pl.BlockSpec((1, B_Q_LORA), lambda i, j: (0, 0)),
                  _resident_spec((B_Q_LORA, tn), lambda i, j: (0, j)),
                  pl.BlockSpec((tm, LANES), lambda i, j: (i, 0)),
                  pl.BlockSpec((tm, LANES), lambda i, j: (i, 0))],
        out_specs=pl.BlockSpec((tm, tn), lambda i, j: (i, j)),
        out_shape=jax.ShapeDtypeStruct((SEQ, n), BF16),
        compiler_params=_params(48, 2),
        name="mla_q_up",
    )(z, g.reshape(1, B_Q_LORA), w, cos_t, sin_t)


def _kvup_kernel(ckv_ref, kr_ref, g_ref, wk_ref, wv_ref, cos_ref, sin_ref,
                 k_ref, v_ref):
    hn = _rms(ckv_ref[...].astype(F32), g_ref[...]).astype(BF16)
    kn = jnp.dot(hn, wk_ref[...], preferred_element_type=F32)
    vn = jnp.dot(hn, wv_ref[...], preferred_element_type=F32)
    kr = _rope128(kr_ref[...].astype(F32), cos_ref[...], sin_ref[...]).astype(BF16)
    lane = lax.broadcasted_iota(jnp.int32, (kn.shape[0], LANES), 1)
    ones_col = jnp.where(lane == 0, 1.0, 0.0).astype(BF16)
    for h in range(B_HEADS):
        k_ref[:, h * B_QK:h * B_QK + LANES] = kn[:, h * B_NOPE:(h + 1) * B_NOPE].astype(BF16)
        k_ref[:, h * B_QK + LANES:(h + 1) * B_QK] = kr
        v_ref[:, h * B_VP:h * B_VP + B_V] = vn[:, h * B_V:(h + 1) * B_V].astype(BF16)
        v_ref[:, h * B_VP + B_V:(h + 1) * B_VP] = ones_col


def _kv_up(z, g, wk, wv, cos_t, sin_t, tm=512):
    return pl.pallas_call(
        _kvup_kernel,
        grid=(SEQ // tm,),
        in_specs=[pl.BlockSpec((tm, B_KV_LORA), lambda i: (i, OFF_CKV // B_KV_LORA)),
                  pl.BlockSpec((tm, LANES), lambda i: (i, OFF_KR // LANES)),
                  pl.BlockSpec((1, B_KV_LORA), lambda i: (0, 0)),
                  pl.BlockSpec((B_KV_LORA, B_HEADS * B_NOPE), lambda i: (0, 0)),
                  pl.BlockSpec((B_KV_LORA, B_WIDTH), lambda i: (0, 0)),
                  pl.BlockSpec((tm, LANES), lambda i: (i, 0)),
                  pl.BlockSpec((tm, LANES), lambda i: (i, 0))],
        out_specs=[pl.BlockSpec((tm, B_HEADS * B_QK), lambda i: (i, 0)),
                   pl.BlockSpec((tm, B_HEADS * B_VP), lambda i: (i, 0))],
        out_shape=[jax.ShapeDtypeStruct((SEQ, B_HEADS * B_QK), BF16),
                   jax.ShapeDtypeStruct((SEQ, B_HEADS * B_VP), BF16)],
        compiler_params=_params(40, 1),
        name="mla_kv_up",
    )(z, z, g.reshape(1, B_KV_LORA), wk, wv, cos_t, sin_t)


def _resident_spec(shape, index_map):
    return pl.BlockSpec(shape, index_map, pipeline_mode=pl.Buffered(1))


def _qk(q, k_ref, k0, tk, k_cols):
    return lax.dot_general(q, k_ref[pl.ds(k0, tk), k_cols], (((1,), (1,)), ((), ())),
                           preferred_element_type=F32)


def _silu(g):
    return g * jax.nn.sigmoid(g)


def _flash_b_kernel(q_ref, k_ref, v_ref, g_ref, o_ref, m_ref, acc_ref, s0_ref, s1_ref,
                    x0_ref, x1_ref, *, tk):
    nk = k_ref.shape[0] // tk
    q = q_ref[...]
    m_ref[...] = jnp.full(m_ref.shape, -jnp.inf, F32)
    acc_ref[...] = jnp.zeros(acc_ref.shape, F32)
    bufs = ((s0_ref, x0_ref), (s1_ref, x1_ref))

    def scores(b, ki):
        s_ref, x_ref = bufs[b]
        s = _qk(q, k_ref, pl.multiple_of(ki * tk, tk), tk, slice(None))
        s_ref[...] = s
        x_ref[...] = jnp.max(s, axis=1, keepdims=True)

    def consume(b, ki):
        s_ref, x_ref = bufs[b]
        m_prev = m_ref[...]
        m_new = jnp.maximum(m_prev, x_ref[...])
        p = jnp.exp2(s_ref[...] - m_new).astype(BF16)
        acc_ref[...] = jnp.exp2(m_prev - m_new) * acc_ref[...] + jnp.dot(
            p, v_ref[pl.ds(pl.multiple_of(ki * tk, tk), tk), :], preferred_element_type=F32)
        m_ref[...] = m_new

    scores(0, 0)

    def pair(j, carry):
        scores(1, 2 * j + 1)
        consume(0, 2 * j)
        scores(0, 2 * j + 2)
        consume(1, 2 * j + 1)
        return carry

    lax.fori_loop(0, nk // 2 - 1, pair, 0, unroll=True)
    scores(1, nk - 1)
    consume(0, nk - 2)
    consume(1, nk - 1)
    acc = acc_ref[...]
    o = acc[:, :B_V] * (1.0 / acc[:, B_V:B_V + 1])
    o_ref[...] = (o * _silu(g_ref[...].astype(F32))).astype(BF16)


def _flash_b(qb, kb, vb, z, tq=512, tk=None):
    tk = tk or FLASH_TK
    return pl.pallas_call(
        functools.partial(_flash_b_kernel, tk=tk),
        grid=(B_HEADS, SEQ // tq),
        in_specs=[pl.BlockSpec((tq, B_QK), lambda h, i: (i, h)),
                  _resident_spec((SEQ, B_QK), lambda h, i: (0, h)),
                  _resident_spec((SEQ, B_VP), lambda h, i: (0, h)),
                  pl.BlockSpec((tq, B_V), lambda h, i: (i, OFF_GB // B_V + h))],
        out_specs=pl.BlockSpec((tq, B_V), lambda h, i: (i, h)),
        out_shape=jax.ShapeDtypeStruct((SEQ, B_WIDTH), BF16),
        scratch_shapes=[pltpu.VMEM((tq, 1), F32), pltpu.VMEM((tq, B_VP), F32),
                        pltpu.VMEM((tq, tk), F32), pltpu.VMEM((tq, tk), F32),
                        pltpu.VMEM((tq, 1), F32), pltpu.VMEM((tq, 1), F32)],
        compiler_params=_params(58, 2),
        name="flash_mla",
    )(qb, kb, vb, z)


def _flash_a_kernel(slopes_ref, laminit_ref, q_ref, k_ref, v_ref, g_ref, subg_ref,
                    lamv_ref, o_ref, m_ref, l_ref, acc_ref, qs_ref, s0_ref, s1_ref,
                    *, tq, tk, scale):
    nk = v_ref.shape[0] // tk
    h = pl.program_id(0)
    q0 = pl.program_id(1) * tq
    kd = q0 // tk
    slope2 = slopes_ref[h] * LOG2E
    col = lax.broadcasted_iota(jnp.int32, (1, tk), 1).astype(F32) * slope2
    row = lax.broadcasted_iota(jnp.int32, (tq, 1), 0)

    def chunk(t):
        return t + (t >= kd).astype(jnp.int32)

    for mp in range(2):
        k_cols = slice(mp * HEAD_DIM, (mp + 1) * HEAD_DIM)
        qs_ref[mp] = (q_ref[:, k_cols].astype(F32) * (scale * LOG2E)).astype(BF16)

    def one_map(mp, outer):
        k_map = k_ref.at[0, mp]
        q = qs_ref[mp]
        m_ref[...] = jnp.full(m_ref.shape, -jnp.inf, F32)
        l_ref[...] = jnp.zeros(l_ref.shape, F32)
        acc_ref[mp] = jnp.zeros(acc_ref.shape[1:], F32)

        def scores(s_ref, ki):
            s_ref[...] = _qk(q, k_map, pl.multiple_of(ki * tk, tk), tk, slice(None))

        def update(t, c, ki):
            m_prev = m_ref[...]
            m_new = jnp.maximum(m_prev, jnp.max(t, axis=1, keepdims=True) + c)
            alpha = jnp.exp2(m_prev - m_new)
            p = jnp.exp2(t - (m_new - c))
            l_ref[...] = alpha * l_ref[...] + jnp.sum(p, axis=1, keepdims=True)
            acc_ref[mp] = alpha * acc_ref[mp] + jnp.dot(
                p.astype(BF16), v_ref[pl.ds(pl.multiple_of(ki * tk, tk), tk), :],
                preferred_element_type=F32)
            m_ref[...] = m_new

        def consume(s_ref, ki):
            sgn = jnp.where(ki < kd, 1.0, -1.0).astype(F32)
            c = (-sgn * slope2) * (row + (q0 - ki * tk)).astype(F32)
            update(s_ref[...] + sgn * col, c, ki)

        def consume_diag(s_ref):
            rel = row - lax.broadcasted_iota(jnp.int32, (tq, tk), 1) + (q0 - kd * tk)
            update(s_ref[...] - slope2 * jnp.abs(rel).astype(F32), jnp.zeros((tq, 1), F32), kd)

        scores(s0_ref, chunk(0))

        def pair(j, carry):
            scores(s1_ref, chunk(2 * j + 1))
            consume(s0_ref, chunk(2 * j))
            scores(s0_ref, chunk(2 * j + 2))
            consume(s1_ref, chunk(2 * j + 1))
            return carry

        lax.fori_loop(0, (nk - 2) // 2, pair, 0, unroll=True)
        scores(s1_ref, kd)
        consume(s0_ref, chunk(nk - 2))
        consume_diag(s1_ref)
        acc_ref[mp] = acc_ref[mp] * (1.0 / l_ref[...])
        return outer

    lax.fori_loop(0, 2, one_map, 0)

    lam_init = laminit_ref[0]
    lv = lamv_ref[...]
    lam = (jnp.exp(jnp.sum(lv[0:1] * lv[1:2], axis=1, keepdims=True))
           - jnp.exp(jnp.sum(lv[2:3] * lv[3:4], axis=1, keepdims=True)) + lam_init)
    o = acc_ref[0] - lam * acc_ref[1]
    o = _rms(o, subg_ref[...]) * (1.0 - lam_init)
    o_ref[...] = (o * _silu(g_ref[...].astype(F32))).astype(BF16)


def _flash_a(z, slopes, lam_init, subln_g, lam_vecs, tq=512, tk=None):
    tk = tk or FLASH_TK
    w = 2 * HEAD_DIM
    smem = pl.BlockSpec(memory_space=pltpu.SMEM)
    ka = z[:, OFF_KA:OFF_KA + A_WIDTH].reshape(SEQ, A_HEADS, 2, HEAD_DIM).transpose(1, 2, 0, 3)
    return pl.pallas_call(
        functools.partial(_flash_a_kernel, tq=tq, tk=tk, scale=HEAD_DIM ** -0.5),
        grid=(A_HEADS, SEQ // tq),
        in_specs=[smem, smem,
                  pl.BlockSpec((tq, w), lambda h, i: (i, OFF_QA // w + h)),
                  _resident_spec((1, 2, SEQ, HEAD_DIM), lambda h, i: (h, 0, 0, 0)),
                  _resident_spec((SEQ, w), lambda h, i: (0, OFF_VA // w + h)),
                  pl.BlockSpec((tq, w), lambda h, i: (i, OFF_GA // w + h)),
                  pl.BlockSpec((1, w), lambda h, i: (0, 0)),
                  pl.BlockSpec((4, HEAD_DIM), lambda h, i: (0, 0))],
        out_specs=pl.BlockSpec((tq, w), lambda h, i: (i, h)),
        out_shape=jax.ShapeDtypeStruct((SEQ, A_WIDTH), BF16),
        scratch_shapes=[pltpu.VMEM((tq, 1), F32), pltpu.VMEM((tq, 1), F32),
                        pltpu.VMEM((2, tq, w), F32), pltpu.VMEM((2, tq, HEAD_DIM), BF16),
                        pltpu.VMEM((tq, tk), F32), pltpu.VMEM((tq, tk), F32)],
        compiler_params=_params(58, 2),
        name="flash_diff",
    )(slopes, lam_init, z, ka, z, z, subln_g.reshape(1, w), lam_vecs)


NA_GROUP_ROWS = 8
NA_Q = NA_GROUP_ROWS * GRID_W
NA_KROWS = 2 * NA_GROUP_ROWS
NA_K = NA_KROWS * GRID_W
NA_KBLK = 256
NA_GROUPS = SEQ // NA_Q
N_DR = 2 * NA_ROWS - 1
N_DC = 2 * NA_COLS - 1
NA_VARIANTS = (
    (0, lambda rq: max(rq - NA_ROWS // 2, 0)),
    (-NA_ROWS // 2, lambda rq: rq),
    (-NA_ROWS, lambda rq: min(rq + NA_ROWS // 2, NA_ROWS)),
)


def _na_bias_kernel(rpb_ref, o_ref):
    h = pl.program_id(0)
    lane = lax.broadcasted_iota(jnp.int32, (GRID_W, LANES), 1)
    c = lax.broadcasted_iota(jnp.int32, (GRID_W, LANES), 0)
    cp = lane & (GRID_W - 1)
    cs = jnp.clip(c - NA_COLS // 2, 0, GRID_W - NA_COLS)
    col_valid = (cp >= cs) & (cp < cs + NA_COLS)
    dc = cp - c + (NA_COLS - 1)
    neg = jnp.full((GRID_W, LANES), NEG, F32)
    base = h * (N_DR * N_DC)
    tabs = []
    for dr in range(N_DR):
        def jb(j, acc, dr=dr):
            return jnp.where(dc == j, rpb_ref[base + dr * N_DC + j], acc)
        t = lax.fori_loop(0, N_DC, jb, neg)
        tabs.append(jnp.where(col_valid, t, neg))
    low = lane < GRID_W
    for var, (delta, rs_fn) in enumerate(NA_VARIANTS):
        for rq in range(NA_GROUP_ROWS):
            rs = rs_fn(rq)
            for pair in range(NA_KROWS // 2):
                halves = []
                for rk in (2 * pair, 2 * pair + 1):
                    if rs <= rk < rs + NA_ROWS:
                        halves.append(tabs[rk - rq + delta + NA_ROWS - 1])
                    else:
                        halves.append(neg)
                o_ref[0, var, rq * GRID_W:(rq + 1) * GRID_W,
                      pair * LANES:(pair + 1) * LANES] = jnp.where(low, halves[0], halves[1])


def _na_bias(rpb):
    return pl.pallas_call(
        _na_bias_kernel,
        grid=(C_HEADS,),
        in_specs=[pl.BlockSpec(memory_space=pltpu.SMEM)],
        out_specs=pl.BlockSpec((1, len(NA_VARIANTS), NA_Q, NA_K), lambda h: (h, 0, 0, 0)),
        out_shape=jax.ShapeDtypeStruct((C_HEADS, len(NA_VARIANTS), NA_Q, NA_K), F32),
        compiler_params=_params(32, 1),
        name="na_bias",
    )(rpb.reshape(-1))


def _na_kernel(q_ref, k0_ref, k1_ref, k2_ref, k3_ref, v0_ref, v1_ref, v2_ref, v3_ref,
               b_ref, g_ref, o_ref, *, scale, heads):
    d = HEAD_DIM
    for hh in range(heads):
        cols = slice(hh * d, (hh + 1) * d)
        q = (q_ref[:, cols].astype(F32) * scale).astype(BF16)
        s = jnp.concatenate(
            [lax.dot_general(q, kr[:, cols], (((1,), (1,)), ((), ())),
                             preferred_element_type=F32)
             for kr in (k0_ref, k1_ref, k2_ref, k3_ref)], axis=1) + b_ref[hh, 0]
        m = jnp.max(s, axis=1, keepdims=True)
        p = jnp.exp(s - m)
        l = jnp.sum(p, axis=1, keepdims=True)
        pb = p.astype(BF16)
        o = None
        for i, vr in enumerate((v0_ref, v1_ref, v2_ref, v3_ref)):
            t = jnp.dot(pb[:, i * NA_KBLK:(i + 1) * NA_KBLK], vr[:, cols],
                        preferred_element_type=F32)
            o = t if o is None else o + t
        o = o * (1.0 / l)
        o_ref[:, cols] = (o * _silu(g_ref[:, cols].astype(F32))).astype(BF16)


def _na(z, bias, heads=4):
    w = heads * HEAD_DIM
    last_start = (SEQ - NA_K) // NA_KBLK

    def kstart(g):
        return jnp.clip(2 * g - 1, 0, last_start)

    def kv_spec(off, i):
        return pl.BlockSpec((NA_KBLK, w), lambda h, g: (kstart(g) + i, off // w + h))

    def variant(g):
        return jnp.where(g == 0, 0, jnp.where(g == NA_GROUPS - 1, 2, 1))

    return pl.pallas_call(
        functools.partial(_na_kernel, scale=HEAD_DIM ** -0.5, heads=heads),
        grid=(C_HEADS // heads, NA_GROUPS),
        in_specs=([pl.BlockSpec((NA_Q, w), lambda h, g: (g, OFF_QC // w + h))]
                  + [kv_spec(OFF_KC, i) for i in range(4)]
                  + [kv_spec(OFF_VC, i) for i in range(4)]
                  + [pl.BlockSpec((heads, 1, NA_Q, NA_K), lambda h, g: (h, variant(g), 0, 0)),
                     pl.BlockSpec((NA_Q, w), lambda h, g: (g, OFF_GC // w + h))]),
        out_specs=pl.BlockSpec((NA_Q, w), lambda h, g: (g, h)),
        out_shape=jax.ShapeDtypeStruct((SEQ, C_WIDTH), BF16),
        compiler_params=_params(56, 2),
        name="na_attn",
    )(z, z, z, z, z, z, z, z, z, bias, z)


def _merge_kernel(ya_ref, yb_ref, yc_ref, wa_ref, wb_ref, wc_ref,
                  sa_ref, sb_ref, sc_ref, o_ref):
    def branch(y_ref, w_ref, s_ref):
        return jax.nn.sigmoid(s_ref[...].astype(F32)) * jnp.dot(
            y_ref[...], w_ref[...], preferred_element_type=F32)
    o_ref[...] = (branch(ya_ref, wa_ref, sa_ref) + branch(yb_ref, wb_ref, sb_ref)
                  + branch(yc_ref, wc_ref, sc_ref)).astype(BF16)


def _merge(ya, yb, yc, wa, wb, wc, z, tm=512, tn=512):
    def y_spec(width):
        return pl.BlockSpec((tm, width), lambda i, j: (i, 0))

    def w_spec(width):
        return pl.BlockSpec((width, tn), lambda i, j: (0, j))

    def s_spec(off):
        return pl.BlockSpec((tm, tn), lambda i, j: (i, off // tn + j))

    return pl.pallas_call(
        _merge_kernel,
        grid=(SEQ // tm, D_MODEL // tn),
        in_specs=[y_spec(A_WIDTH), y_spec(B_WIDTH), y_spec(C_WIDTH),
                  w_spec(A_WIDTH), w_spec(B_WIDTH), w_spec(C_WIDTH),
                  s_spec(OFF_SA), s_spec(OFF_SB), s_spec(OFF_SC)],
        out_specs=pl.BlockSpec((tm, tn), lambda i, j: (i, j)),
        out_shape=jax.ShapeDtypeStruct((SEQ, D_MODEL), BF16),
        compiler_params=_params(48, 2),
        name="merge",
    )(ya, yb, yc, wa, wb, wc, z, z, z)


def _swap_halves(w):
    half = w.shape[-1] // 2
    return jnp.concatenate([w[..., half:], w[..., :half]], axis=-1)


def _prep_w_in(w):
    o = np.cumsum((0,) + (A_WIDTH,) * 4 + (B_Q_LORA, B_KV_LORA, B_ROPE, B_WIDTH)
                  + (C_WIDTH,) * 4 + (D_MODEL,) * 3)
    a_all = w[..., o[0]:o[4]]
    cq = w[..., o[4]:o[5]]
    ckv = w[..., o[5]:o[6]]
    kr = w[..., o[6]:o[7]]
    gb = w[..., o[7]:o[8]]
    c_all = w[..., o[8]:o[12]]
    gates = w[..., o[12]:o[15]]
    pad = jnp.zeros(w.shape[:-1] + (Z_COLS - Z_USED,), w.dtype)
    return jnp.concatenate([cq, ckv, a_all, gates, c_all, gb, kr, _swap_halves(kr), pad],
                           axis=-1).astype(BF16)


def _prep_w_uq(w):
    w = w.reshape(w.shape[:-1] + (B_HEADS, B_NOPE + B_ROPE))
    rope = w[..., B_NOPE:]
    w = jnp.concatenate([w[..., :B_NOPE], rope, _swap_halves(rope)], axis=-1)
    return w.reshape(w.shape[:-2] + (B_HEADS * B_QK,)).astype(BF16)


def _prep_w_ukv(w):
    w = w.reshape(w.shape[:-1] + (B_HEADS, B_NOPE + B_V))
    wk = w[..., :B_NOPE].reshape(w.shape[:-2] + (B_HEADS * B_NOPE,)).astype(BF16)
    wv = w[..., B_NOPE:].reshape(w.shape[:-2] + (B_WIDTH,)).astype(BF16)
    return wk, wv


def _rope_tables():
    inv_freq = ROPE_BASE ** (-jnp.arange(0, B_ROPE, 2, dtype=F32) / B_ROPE)
    ang = jnp.arange(SEQ, dtype=F32)[:, None] * inv_freq[None, :]
    cos, sin = jnp.cos(ang), jnp.sin(ang)
    zero = jnp.zeros((SEQ, LANES - B_ROPE), F32)
    return (jnp.concatenate([cos, cos, zero], axis=1),
            jnp.concatenate([-sin, sin, zero], axis=1))


def kernel(x, norm_g, w_in, a_lam_q1, a_lam_k1, a_lam_q2, a_lam_k2, a_subln_g,
           b_q_norm_g, b_kv_norm_g, b_w_uq, b_w_ukv, c_rpb, w_br_a, w_br_b, w_br_c,
           w_o, final_norm_g):
    assert x.shape == (1, SEQ, D_MODEL)
    cos_t, sin_t = _rope_tables()
    slopes = jnp.asarray(2.0 ** (-8.0 * np.arange(1, A_HEADS + 1) / A_HEADS), dtype=F32)
    lam_inits = jnp.asarray([0.8 - 0.6 * math.exp(-0.3 * l) for l in range(DEPTH)],
                            dtype=F32).reshape(DEPTH, 1)
    lam_vecs = jnp.stack([a_lam_q1, a_lam_k1, a_lam_q2, a_lam_k2], axis=1).astype(F32)
    layers = jnp.arange(DEPTH, dtype=jnp.int32).reshape(DEPTH, 1)
    w_in_all = _prep_w_in(w_in)
    wk_all, wv_all = _prep_w_ukv(b_w_ukv)

    def layer(xc, p):
        (l_idx, g_l, lam_init_l, lam_vecs_l, subg_l, qg_l, kvg_l, wuq_l, wk_l, wv_l,
         rpb_l, wa_l, wb_l, wc_l, wo_l) = p
        h = _rmsnorm(xc, g_l, BF16)
        z = _in_proj(h, w_in_all, l_idx)
        ya = _flash_a(z, slopes, lam_init_l, subg_l, lam_vecs_l)
        qb = _q_up(z, qg_l, wuq_l, cos_t, sin_t)
        kb, vb = _kv_up(z, kvg_l, wk_l, wv_l, cos_t, sin_t)
        yb = _flash_b(qb, kb, vb, z)
        yc = _na(z, _na_bias(rpb_l))
        merged = _merge(ya, yb, yc, wa_l, wb_l, wc_l, z)
        return _matmul_residual(merged, wo_l, xc, 512, 1024), None

    xs = (layers, norm_g, lam_inits, lam_vecs, a_subln_g, b_q_norm_g, b_kv_norm_g,
          _prep_w_uq(b_w_uq), wk_all, wv_all, c_rpb, w_br_a.astype(BF16),
          w_br_b.astype(BF16), w_br_c.astype(BF16), w_o.astype(BF16))
    xf, _ = lax.scan(layer, x[0], xs)
    return _rmsnorm(xf, final_norm_g, F32)[None]
```

```python
import functools
import math

import numpy as np
import jax
import jax.numpy as jnp
from jax import lax
from jax.experimental import pallas as pl
from jax.experimental.pallas import tpu as pltpu

F32 = jnp.float32
BF16 = jnp.bfloat16

D_MODEL = 4096
SEQ = 16384
DEPTH = 4
GRID_W = 64
HEAD_DIM = 128
A_HEADS = 8
A_WIDTH = A_HEADS * 2 * HEAD_DIM
B_HEADS = 16
B_Q_LORA = 1536
B_KV_LORA = 512
B_NOPE = 128
B_ROPE = 64
B_V = 128
B_WIDTH = B_HEADS * B_V
C_HEADS = 16
C_WIDTH = C_HEADS * HEAD_DIM
NA_ROWS = 8
NA_COLS = 16
ROPE_BASE = 10000.0
EPS = 1e-6

LANES = 128
MIB = 1024 * 1024
NEG = -1e30

OFF_CQ = 0
OFF_CKV = OFF_CQ + B_Q_LORA
OFF_QA = OFF_CKV + B_KV_LORA
OFF_KA = OFF_QA + A_WIDTH
OFF_VA = OFF_KA + A_WIDTH
OFF_GA = OFF_VA + A_WIDTH
OFF_SA = OFF_GA + A_WIDTH
OFF_SB = OFF_SA + D_MODEL
OFF_SC = OFF_SB + D_MODEL
OFF_QC = OFF_SC + D_MODEL
OFF_KC = OFF_QC + C_WIDTH
OFF_VC = OFF_KC + C_WIDTH
OFF_GC = OFF_VC + C_WIDTH
OFF_GB = OFF_GC + C_WIDTH
OFF_KR = OFF_GB + B_WIDTH
Z_USED = OFF_KR + 2 * B_ROPE
Z_TN = 512
Z_COLS = -(-Z_USED // Z_TN) * Z_TN

B_QK = 2 * LANES
B_VP = 2 * LANES
LOG2E = math.log2(math.e)
FLASH_TK = 2048


def _params(vmem_mib, ndims):
    return pltpu.CompilerParams(dimension_semantics=("arbitrary",) * ndims,
                                vmem_limit_bytes=vmem_mib * MIB)


def _rms(x, g):
    ms = jnp.mean(x * x, axis=-1, keepdims=True)
    return x * lax.rsqrt(ms + EPS) * g


def _rmsnorm_kernel(x_ref, g_ref, o_ref):
    o_ref[...] = _rms(x_ref[...], g_ref[...]).astype(o_ref.dtype)


def _rmsnorm(x, g, out_dtype, tm=256):
    m, d = x.shape
    return pl.pallas_call(
        _rmsnorm_kernel,
        grid=(m // tm,),
        in_specs=[pl.BlockSpec((tm, d), lambda i: (i, 0)),
                  pl.BlockSpec((1, d), lambda i: (0, 0))],
        out_specs=pl.BlockSpec((tm, d), lambda i: (i, 0)),
        out_shape=jax.ShapeDtypeStruct((m, d), out_dtype),
        compiler_params=_params(40, 1),
        name="rmsnorm",
    )(x, g.reshape(1, d))


def _in_proj_kernel(layer_ref, a_ref, b_ref, o_ref):
    del layer_ref
    o_ref[...] = jnp.dot(a_ref[...], b_ref[...],
                         preferred_element_type=F32).astype(o_ref.dtype)


def _in_proj(h, w_all, layer, tm=1024, tn=Z_TN):
    m, k = h.shape
    n = w_all.shape[2]
    return pl.pallas_call(
        _in_proj_kernel,
        grid_spec=pltpu.PrefetchScalarGridSpec(
            num_scalar_prefetch=1,
            grid=(m // tm, n // tn),
            in_specs=[pl.BlockSpec((tm, k), lambda i, j, l: (i, 0)),
                      pl.BlockSpec((None, k, tn), lambda i, j, l: (l[0], 0, j))],
            out_specs=pl.BlockSpec((tm, tn), lambda i, j, l: (i, j))),
        out_shape=jax.ShapeDtypeStruct((m, n), BF16),
        compiler_params=_params(48, 2),
        name="in_proj",
    )(layer, h, w_all)


def _mm_res_kernel(a_ref, b_ref, x_ref, o_ref):
    o_ref[...] = x_ref[...] + jnp.dot(a_ref[...], b_ref[...],
                                      preferred_element_type=F32)


def _matmul_residual(a, b, x, tm, tn):
    m, k = a.shape
    n = b.shape[1]
    return pl.pallas_call(
        _mm_res_kernel,
        grid=(m // tm, n // tn),
        in_specs=[pl.BlockSpec((tm, k), lambda i, j: (i, 0)),
                  pl.BlockSpec((k, tn), lambda i, j: (0, j)),
                  pl.BlockSpec((tm, tn), lambda i, j: (i, j))],
        out_specs=pl.BlockSpec((tm, tn), lambda i, j: (i, j)),
        out_shape=jax.ShapeDtypeStruct((m, n), F32),
        compiler_params=_params(48, 2),
        name="out_proj",
    )(a, b, x)


def _rope128(u, cos_t, sin_t):
    return u * cos_t + pltpu.roll(u, 2 * (B_ROPE // 2), 1) * sin_t


def _qup_kernel(cq_ref, g_ref, w_ref, cos_ref, sin_ref, o_ref, *, heads, scale):
    hn = _rms(cq_ref[...].astype(F32), g_ref[...]).astype(BF16)
    acc = jnp.dot(hn, w_ref[...], preferred_element_type=F32)
    cos_t = cos_ref[...]
    sin_t = sin_ref[...]
    for h in range(heads):
        lo = acc[:, h * B_QK:h * B_QK + LANES]
        up = acc[:, h * B_QK + LANES:(h + 1) * B_QK]
        o_ref[:, h * B_QK:h * B_QK + LANES] = (lo * scale).astype(BF16)
        o_ref[:, h * B_QK + LANES:(h + 1) * B_QK] = (
            _rope128(up, cos_t, sin_t) * scale).astype(BF16)


def _q_up(z, g, w, cos_t, sin_t, tm=512, heads=B_HEADS):
    tn = heads * B_QK
    n = B_HEADS * B_QK
    scale = (B_NOPE + B_ROPE) ** -0.5 * LOG2E
    return pl.pallas_call(
        functools.partial(_qup_kernel, heads=heads, scale=scale),
        grid=(SEQ // tm, n // tn),
        in_specs=[pl.BlockSpec((tm, B_Q_LORA), lambda i, j: (i, OFF_CQ // B_Q_LORA)),
                  pl.BlockSpec((1, B_Q_LORA), lambda i, j: (0, 0)),
                  _resident_spec((B_Q_LORA, tn), lambda i, j: (0, j)),
                  pl.BlockSpec((tm, LANES), lambda i, j: (i, 0)),
                  pl.BlockSpec((tm, LANES), lambda i, j: (i, 0))],
        out_specs=pl.BlockSpec((tm, tn), lambda i, j: (i, j)),
        out_shape=jax.ShapeDtypeStruct((SEQ, n), BF16),
        compiler_params=_params(48, 2),
        name="mla_q_up",
    )(z, g.reshape(1, B_Q_LORA), w, cos_t, sin_t)


def _kvup_kernel(ckv_ref, kr_ref, g_ref, wk_ref, wv_ref, cos_ref, sin_ref,
                 k_ref, v_ref):
    hn = _rms(ckv_ref[...].astype(F32), g_ref[...]).astype(BF16)
    kn = jnp.dot(hn, wk_ref[...], preferred_element_type=F32)
    vn = jnp.dot(hn, wv_ref[...], preferred_element_type=F32)
    kr = _rope128(kr_ref[...].astype(F32), cos_ref[...], sin_ref[...]).astype(BF16)
    lane = lax.broadcasted_iota(jnp.int32, (kn.shape[0], LANES), 1)
    ones_col = jnp.where(lane == 0, 1.0, 0.0).astype(BF16)
    for h in range(B_HEADS):
        k_ref[:, h * B_QK:h * B_QK + LANES] = kn[:, h * B_NOPE:(h + 1) * B_NOPE].astype(BF16)
        k_ref[:, h * B_QK + LANES:(h + 1) * B_QK] = kr
        v_ref[:, h * B_VP:h * B_VP + B_V] = vn[:, h * B_V:(h + 1) * B_V].astype(BF16)
        v_ref[:, h * B_VP + B_V:(h + 1) * B_VP] = ones_col


def _kv_up(z, g, wk, wv, cos_t, sin_t, tm=512):
    return pl.pallas_call(
        _kvup_kernel,
        grid=(SEQ // tm,),
        in_specs=[pl.BlockSpec((tm, B_KV_LORA), lambda i: (i, OFF_CKV // B_KV_LORA)),
                  pl.BlockSpec((tm, LANES), lambda i: (i, OFF_KR // LANES)),
                  pl.BlockSpec((1, B_KV_LORA), lambda i: (0, 0)),
                  pl.BlockSpec((B_KV_LORA, B_HEADS * B_NOPE), lambda i: (0, 0)),
                  pl.BlockSpec((B_KV_LORA, B_WIDTH), lambda i: (0, 0)),
                  pl.BlockSpec((tm, LANES), lambda i: (i, 0)),
                  pl.BlockSpec((tm, LANES), lambda i: (i, 0))],
        out_specs=[pl.BlockSpec((tm, B_HEADS * B_QK), lambda i: (i, 0)),
                   pl.BlockSpec((tm, B_HEADS * B_VP), lambda i: (i, 0))],
        out_shape=[jax.ShapeDtypeStruct((SEQ, B_HEADS * B_QK), BF16),
                   jax.ShapeDtypeStruct((SEQ, B_HEADS * B_VP), BF16)],
        compiler_params=_params(40, 1),
        name="mla_kv_up",
    )(z, z, g.reshape(1, B_KV_LORA), wk, wv, cos_t, sin_t)


def _resident_spec(shape, index_map):
    return pl.BlockSpec(shape, index_map, pipeline_mode=pl.Buffered(1))


def _qk(q, k_ref, k0, tk, k_cols):
    return lax.dot_general(q, k_ref[pl.ds(k0, tk), k_cols], (((1,), (1,)), ((), ())),
                           preferred_element_type=F32)


def _silu(g):
    return g * jax.nn.sigmoid(g)


def _flash_b_kernel(q_ref, k_ref, v_ref, g_ref, o_ref, m_ref, acc_ref, s0_ref, s1_ref,
                    x0_ref, x1_ref, *, tk):
    nk = k_ref.shape[0] // tk
    q = q_ref[...]
    m_ref[...] = jnp.full(m_ref.shape, -jnp.inf, F32)
    acc_ref[...] = jnp.zeros(acc_ref.shape, F32)
    bufs = ((s0_ref, x0_ref), (s1_ref, x1_ref))

    def scores(b, ki):
        s_ref, x_ref = bufs[b]
        s = _qk(q, k_ref, pl.multiple_of(ki * tk, tk), tk, slice(None))
        s_ref[...] = s
        x_ref[...] = jnp.max(s, axis=1, keepdims=True)

    def consume(b, ki):
        s_ref, x_ref = bufs[b]
        m_prev = m_ref[...]
        m_new = jnp.maximum(m_prev, x_ref[...])
        p = jnp.exp2(s_ref[...] - m_new).astype(BF16)
        acc_ref[...] = jnp.exp2(m_prev - m_new) * acc_ref[...] + jnp.dot(
            p, v_ref[pl.ds(pl.multiple_of(ki * tk, tk), tk), :], preferred_element_type=F32)
        m_ref[...] = m_new

    scores(0, 0)

    def pair(j, carry):
        scores(1, 2 * j + 1)
        consume(0, 2 * j)
        scores(0, 2 * j + 2)
        consume(1, 2 * j + 1)
        return carry

    lax.fori_loop(0, nk // 2 - 1, pair, 0, unroll=True)
    scores(1, nk - 1)
    consume(0, nk - 2)
    consume(1, nk - 1)
    acc = acc_ref[...]
    o = acc[:, :B_V] * (1.0 / acc[:, B_V:B_V + 1])
    o_ref[...] = (o * _silu(g_ref[...].astype(F32))).astype(BF16)


def _flash_b(qb, kb, vb, z, tq=512, tk=None):
    tk = tk or FLASH_TK
    return pl.pallas_call(
        functools.partial(_flash_b_kernel, tk=tk),
        grid=(B_HEADS, SEQ // tq),
        in_specs=[pl.BlockSpec((tq, B_QK), lambda h, i: (i, h)),
                  _resident_spec((SEQ, B_QK), lambda h, i: (0, h)),
                  _resident_spec((SEQ, B_VP), lambda h, i: (0, h)),
                  pl.BlockSpec((tq, B_V), lambda h, i: (i, OFF_GB // B_V + h))],
        out_specs=pl.BlockSpec((tq, B_V), lambda h, i: (i, h)),
        out_shape=jax.ShapeDtypeStruct((SEQ, B_WIDTH), BF16),
        scratch_shapes=[pltpu.VMEM((tq, 1), F32), pltpu.VMEM((tq, B_VP), F32),
                        pltpu.VMEM((tq, tk), F32), pltpu.VMEM((tq, tk), F32),
                        pltpu.VMEM((tq, 1), F32), pltpu.VMEM((tq, 1), F32)],
        compiler_params=_params(58, 2),
        name="flash_mla",
    )(qb, kb, vb, z)


def _flash_a_kernel(slopes_ref, laminit_ref, q_ref, k_ref, v_ref, g_ref, subg_ref,
                    lamv_ref, o_ref, m_ref, l_ref, acc_ref, qs_ref, s0_ref, s1_ref,
                    *, tq, tk, scale):
    nk = v_ref.shape[0] // tk
    h = pl.program_id(0)
    q0 = pl.program_id(1) * tq
    kd = q0 // tk
    slope2 = slopes_ref[h] * LOG2E
    col = lax.broadcasted_iota(jnp.int32, (1, tk), 1).astype(F32) * slope2
    row = lax.broadcasted_iota(jnp.int32, (tq, 1), 0)

    def chunk(t):
        return t + (t >= kd).astype(jnp.int32)

    for mp in range(2):
        k_cols = slice(mp * HEAD_DIM, (mp + 1) * HEAD_DIM)
        qs_ref[mp] = (q_ref[:, k_cols].astype(F32) * (scale * LOG2E)).astype(BF16)

    def one_map(mp, outer):
        k_map = k_ref.at[0, mp]
        q = qs_ref[mp]
        m_ref[...] = jnp.full(m_ref.shape, -jnp.inf, F32)
        l_ref[...] = jnp.zeros(l_ref.shape, F32)
        acc_ref[mp] = jnp.zeros(acc_ref.shape[1:], F32)

        def scores(s_ref, ki):
            s_ref[...] = _qk(q, k_map, pl.multiple_of(ki * tk, tk), tk, slice(None))

        def update(t, c, ki):
            m_prev = m_ref[...]
            m_new = jnp.maximum(m_prev, jnp.max(t, axis=1, keepdims=True) + c)
            alpha = jnp.exp2(m_prev - m_new)
            p = jnp.exp2(t - (m_new - c))
            l_ref[...] = alpha * l_ref[...] + jnp.sum(p, axis=1, keepdims=True)
            acc_ref[mp] = alpha * acc_ref[mp] + jnp.dot(
                p.astype(BF16), v_ref[pl.ds(pl.multiple_of(ki * tk, tk), tk), :],
                preferred_element_type=F32)
            m_ref[...] = m_new

        def consume(s_ref, ki):
            sgn = jnp.where(ki < kd, 1.0, -1.0).astype(F32)
            c = (-sgn * slope2) * (row + (q0 - ki * tk)).astype(F32)
            update(s_ref[...] + sgn * col, c, ki)

        def consume_diag(s_ref):
            rel = row - lax.broadcasted_iota(jnp.int32, (tq, tk), 1) + (q0 - kd * tk)
            update(s_ref[...] - slope2 * jnp.abs(rel).astype(F32), jnp.zeros((tq, 1), F32), kd)

        scores(s0_ref, chunk(0))

        def pair(j, carry):
            scores(s1_ref, chunk(2 * j + 1))
            consume(s0_ref, chunk(2 * j))
            scores(s0_ref, chunk(2 * j + 2))
            consume(s1_ref, chunk(2 * j + 1))
            return carry

        lax.fori_loop(0, (nk - 2) // 2, pair, 0, unroll=True)
        scores(s1_ref, kd)
        consume(s0_ref, chunk(nk - 2))
        consume_diag(s1_ref)
        acc_ref[mp] = acc_ref[mp] * (1.0 / l_ref[...])
        return outer

    lax.fori_loop(0, 2, one_map, 0)

    lam_init = laminit_ref[0]
    lv = lamv_ref[...]
    lam = (jnp.exp(jnp.sum(lv[0:1] * lv[1:2], axis=1, keepdims=True))
           - jnp.exp(jnp.sum(lv[2:3] * lv[3:4], axis=1, keepdims=True)) + lam_init)
    o = acc_ref[0] - lam * acc_ref[1]
    o = _rms(o, subg_ref[...]) * (1.0 - lam_init)
    o_ref[...] = (o * _silu(g_ref[...].astype(F32))).astype(BF16)


def _flash_a(z, slopes, lam_init, subln_g, lam_vecs, tq=512, tk=None):
    tk = tk or FLASH_TK
    w = 2 * HEAD_DIM
    smem = pl.BlockSpec(memory_space=pltpu.SMEM)
    ka = z[:, OFF_KA:OFF_KA + A_WIDTH].reshape(SEQ, A_HEADS, 2, HEAD_DIM).transpose(1, 2, 0, 3)
    return pl.pallas_call(
        functools.partial(_flash_a_kernel, tq=tq, tk=tk, scale=HEAD_DIM ** -0.5),
        grid=(A_HEADS, SEQ // tq),
        in_specs=[smem, smem,
                  pl.BlockSpec((tq, w), lambda h, i: (i, OFF_QA // w + h)),
                  _resident_spec((1, 2, SEQ, HEAD_DIM), lambda h, i: (h, 0, 0, 0)),
                  _resident_spec((SEQ, w), lambda h, i: (0, OFF_VA // w + h)),
                  pl.BlockSpec((tq, w), lambda h, i: (i, OFF_GA // w + h)),
                  pl.BlockSpec((1, w), lambda h, i: (0, 0)),
                  pl.BlockSpec((4, HEAD_DIM), lambda h, i: (0, 0))],
        out_specs=pl.BlockSpec((tq, w), lambda h, i: (i, h)),
        out_shape=jax.ShapeDtypeStruct((SEQ, A_WIDTH), BF16),
        scratch_shapes=[pltpu.VMEM((tq, 1), F32), pltpu.VMEM((tq, 1), F32),
                        pltpu.VMEM((2, tq, w), F32), pltpu.VMEM((2, tq, HEAD_DIM), BF16),
                        pltpu.VMEM((tq, tk), F32), pltpu.VMEM((tq, tk), F32)],
        compiler_params=_params(58, 2),
        name="flash_diff",
    )(slopes, lam_init, z, ka, z, z, subln_g.reshape(1, w), lam_vecs)


NA_GROUP_ROWS = 8
NA_Q = NA_GROUP_ROWS * GRID_W
NA_KROWS = 2 * NA_GROUP_ROWS
NA_K = NA_KROWS * GRID_W
NA_KBLK = 256
NA_GROUPS = SEQ // NA_Q
N_DR = 2 * NA_ROWS - 1
N_DC = 2 * NA_COLS - 1
NA_VARIANTS = (
    (0, lambda rq: max(rq - NA_ROWS // 2, 0)),
    (-NA_ROWS // 2, lambda rq: rq),
    (-NA_ROWS, lambda rq: min(rq + NA_ROWS // 2, NA_ROWS)),
)


def _na_bias_kernel(rpb_ref, o_ref):
    h = pl.program_id(0)
    lane = lax.broadcasted_iota(jnp.int32, (GRID_W, LANES), 1)
    c = lax.broadcasted_iota(jnp.int32, (GRID_W, LANES), 0)
    cp = lane & (GRID_W - 1)
    cs = jnp.clip(c - NA_COLS // 2, 0, GRID_W - NA_COLS)
    col_valid = (cp >= cs) & (cp < cs + NA_COLS)
    dc = cp - c + (NA_COLS - 1)
    neg = jnp.full((GRID_W, LANES), NEG, F32)
    base = h * (N_DR * N_DC)
    tabs = []
    for dr in range(N_DR):
        def jb(j, acc, dr=dr):
            return jnp.where(dc == j, rpb_ref[base + dr * N_DC + j], acc)
        t = lax.fori_loop(0, N_DC, jb, neg)
        tabs.append(jnp.where(col_valid, t, neg))
    low = lane < GRID_W
    for var, (delta, rs_fn) in enumerate(NA_VARIANTS):
        for rq in range(NA_GROUP_ROWS):
            rs = rs_fn(rq)
            for pair in range(NA_KROWS // 2):
                halves = []
                for rk in (2 * pair, 2 * pair + 1):
                    if rs <= rk < rs + NA_ROWS:
                        halves.append(tabs[rk - rq + delta + NA_ROWS - 1])
                    else:
                        halves.append(neg)
                o_ref[0, var, rq * GRID_W:(rq + 1) * GRID_W,
                      pair * LANES:(pair + 1) * LANES] = jnp.where(low, halves[0], halves[1])


def _na_bias(rpb):
    return pl.pallas_call(
        _na_bias_kernel,
        grid=(C_HEADS,),
        in_specs=[pl.BlockSpec(memory_space=pltpu.SMEM)],
        out_specs=pl.BlockSpec((1, len(NA_VARIANTS), NA_Q, NA_K), lambda h: (h, 0, 0, 0)),
        out_shape=jax.ShapeDtypeStruct((C_HEADS, len(NA_VARIANTS), NA_Q, NA_K), F32),
        compiler_params=_params(32, 1),
        name="na_bias",
    )(rpb.reshape(-1))


def _na_kernel(q_ref, k0_ref, k1_ref, k2_ref, k3_ref, v0_ref, v1_ref, v2_ref, v3_ref,
               b_ref, g_ref, o_ref, *, scale, heads):
    d = HEAD_DIM
    for hh in range(heads):
        cols = slice(hh * d, (hh + 1) * d)
        q = (q_ref[:, cols].astype(F32) * scale).astype(BF16)
        s = jnp.concatenate(
            [lax.dot_general(q, kr[:, cols], (((1,), (1,)), ((), ())),
                             preferred_element_type=F32)
             for kr in (k0_ref, k1_ref, k2_ref, k3_ref)], axis=1) + b_ref[hh, 0]
        m = jnp.max(s, axis=1, keepdims=True)
        p = jnp.exp(s - m)
        l = jnp.sum(p, axis=1, keepdims=True)
        pb = p.astype(BF16)
        o = None
        for i, vr in enumerate((v0_ref, v1_ref, v2_ref, v3_ref)):
            t = jnp.dot(pb[:, i * NA_KBLK:(i + 1) * NA_KBLK], vr[:, cols],
                        preferred_element_type=F32)
            o = t if o is None else o + t
        o = o * (1.0 / l)
        o_ref[:, cols] = (o * _silu(g_ref[:, cols].astype(F32))).astype(BF16)


def _na(z, bias, heads=4):
    w = heads * HEAD_DIM
    last_start = (SEQ - NA_K) // NA_KBLK

    def kstart(g):
        return jnp.clip(2 * g - 1, 0, last_start)

    def kv_spec(off, i):
        return pl.BlockSpec((NA_KBLK, w), lambda h, g: (kstart(g) + i, off // w + h))

    def variant(g):
        return jnp.where(g == 0, 0, jnp.where(g == NA_GROUPS - 1, 2, 1))

    return pl.pallas_call(
        functools.partial(_na_kernel, scale=HEAD_DIM ** -0.5, heads=heads),
        grid=(C_HEADS // heads, NA_GROUPS),
        in_specs=([pl.BlockSpec((NA_Q, w), lambda h, g: (g, OFF_QC // w + h))]
                  + [kv_spec(OFF_KC, i) for i in range(4)]
                  + [kv_spec(OFF_VC, i) for i in range(4)]
                  + [pl.BlockSpec((heads, 1, NA_Q, NA_K), lambda h, g: (h, variant(g), 0, 0)),
                     pl.BlockSpec((NA_Q, w), lambda h, g: (g, OFF_GC // w + h))]),
        out_specs=pl.BlockSpec((NA_Q, w), lambda h, g: (g, h)),
        out_shape=jax.ShapeDtypeStruct((SEQ, C_WIDTH), BF16),
        compiler_params=_params(56, 2),
        name="na_attn",
    )(z, z, z, z, z, z, z, z, z, bias, z)


def _merge_kernel(ya_ref, yb_ref, yc_ref, wa_ref, wb_ref, wc_ref,
                  sa_ref, sb_ref, sc_ref, o_ref):
    def branch(y_ref, w_ref, s_ref):
        return jax.nn.sigmoid(s_ref[...].astype(F32)) * jnp.dot(
            y_ref[...], w_ref[...], preferred_element_type=F32)
    o_ref[...] = (branch(ya_ref, wa_ref, sa_ref) + branch(yb_ref, wb_ref, sb_ref)
                  + branch(yc_ref, wc_ref, sc_ref)).astype(BF16)


def _merge(ya, yb, yc, wa, wb, wc, z, tm=512, tn=1024):
    def y_spec(width):
        return pl.BlockSpec((tm, width), lambda i, j: (i, 0))

    def w_spec(width):
        return pl.BlockSpec((width, tn), lambda i, j: (0, j))

    def s_spec(off):
        return pl.BlockSpec((tm, tn), lambda i, j: (i, off // tn + j))

    return pl.pallas_call(
        _merge_kernel,
        grid=(SEQ // tm, D_MODEL // tn),
        in_specs=[y_spec(A_WIDTH), y_spec(B_WIDTH), y_spec(C_WIDTH),
                  w_spec(A_WIDTH), w_spec(B_WIDTH), w_spec(C_WIDTH),
                  s_spec(OFF_SA), s_spec(OFF_SB), s_spec(OFF_SC)],
        out_specs=pl.BlockSpec((tm, tn), lambda i, j: (i, j)),
        out_shape=jax.ShapeDtypeStruct((SEQ, D_MODEL), BF16),
        compiler_params=_params(58, 2),
        name="merge",
    )(ya, yb, yc, wa, wb, wc, z, z, z)


def _swap_halves(w):
    half = w.shape[-1] // 2
    return jnp.concatenate([w[..., half:], w[..., :half]], axis=-1)


def _prep_w_in(w):
    o = np.cumsum((0,) + (A_WIDTH,) * 4 + (B_Q_LORA, B_KV_LORA, B_ROPE, B_WIDTH)
                  + (C_WIDTH,) * 4 + (D_MODEL,) * 3)
    a_all = w[..., o[0]:o[4]]
    cq = w[..., o[4]:o[5]]
    ckv = w[..., o[5]:o[6]]
    kr = w[..., o[6]:o[7]]
    gb = w[..., o[7]:o[8]]
    c_all = w[..., o[8]:o[12]]
    gates = w[..., o[12]:o[15]]
    pad = jnp.zeros(w.shape[:-1] + (Z_COLS - Z_USED,), w.dtype)
    return jnp.concatenate([cq, ckv, a_all, gates, c_all, gb, kr, _swap_halves(kr), pad],
                           axis=-1).astype(BF16)


def _prep_w_uq(w):
    w = w.reshape(w.shape[:-1] + (B_HEADS, B_NOPE + B_ROPE))
    rope = w[..., B_NOPE:]
    w = jnp.concatenate([w[..., :B_NOPE], rope, _swap_halves(rope)], axis=-1)
    return w.reshape(w.shape[:-2] + (B_HEADS * B_QK,)).astype(BF16)


def _prep_w_ukv(w):
    w = w.reshape(w.shape[:-1] + (B_HEADS, B_NOPE + B_V))
    wk = w[..., :B_NOPE].reshape(w.shape[:-2] + (B_HEADS * B_NOPE,)).astype(BF16)
    wv = w[..., B_NOPE:].reshape(w.shape[:-2] + (B_WIDTH,)).astype(BF16)
    return wk, wv


def _rope_tables():
    inv_freq = ROPE_BASE ** (-jnp.arange(0, B_ROPE, 2, dtype=F32) / B_ROPE)
    ang = jnp.arange(SEQ, dtype=F32)[:, None] * inv_freq[None, :]
    cos, sin = jnp.cos(ang), jnp.sin(ang)
    zero = jnp.zeros((SEQ, LANES - B_ROPE), F32)
    return (jnp.concatenate([cos, cos, zero], axis=1),
            jnp.concatenate([-sin, sin, zero], axis=1))


def kernel(x, norm_g, w_in, a_lam_q1, a_lam_k1, a_lam_q2, a_lam_k2, a_subln_g,
           b_q_norm_g, b_kv_norm_g, b_w_uq, b_w_ukv, c_rpb, w_br_a, w_br_b, w_br_c,
           w_o, final_norm_g):
    assert x.shape == (1, SEQ, D_MODEL)
    cos_t, sin_t = _rope_tables()
    slopes = jnp.asarray(2.0 ** (-8.0 * np.arange(1, A_HEADS + 1) / A_HEADS), dtype=F32)
    lam_inits = jnp.asarray([0.8 - 0.6 * math.exp(-0.3 * l) for l in range(DEPTH)],
                            dtype=F32).reshape(DEPTH, 1)
    lam_vecs = jnp.stack([a_lam_q1, a_lam_k1, a_lam_q2, a_lam_k2], axis=1).astype(F32)
    layers = jnp.arange(DEPTH, dtype=jnp.int32).reshape(DEPTH, 1)
    w_in_all = _prep_w_in(w_in)
    wk_all, wv_all = _prep_w_ukv(b_w_ukv)

    def layer(xc, p):
        (l_idx, g_l, lam_init_l, lam_vecs_l, subg_l, qg_l, kvg_l, wuq_l, wk_l, wv_l,
         rpb_l, wa_l, wb_l, wc_l, wo_l) = p
        h = _rmsnorm(xc, g_l, BF16)
        z = _in_proj(h, w_in_all, l_idx)
        ya = _flash_a(z, slopes, lam_init_l, subg_l, lam_vecs_l)
        qb = _q_up(z, qg_l, wuq_l, cos_t, sin_t)
        kb, vb = _kv_up(z, kvg_l, wk_l, wv_l, cos_t, sin_t)
        yb = _flash_b(qb, kb, vb, z)
        yc = _na(z, _na_bias(rpb_l))
        merged = _merge(ya, yb, yc, wa_l, wb_l, wc_l, z)
        return _matmul_residual(merged, wo_l, xc, 512, 1024), None

    xs = (layers, norm_g, lam_inits, lam_vecs, a_subln_g, b_q_norm_g, b_kv_norm_g,
          _prep_w_uq(b_w_uq), wk_all, wv_all, c_rpb, w_br_a.astype(BF16),
          w_br_b.astype(BF16), w_br_c.astype(BF16), w_o.astype(BF16))
    xf, _ = lax.scan(layer, x[0], xs)
    return _rmsnorm(xf, final_norm_g, F32)[None]
```
